```python
import math
import jax
import jax.numpy as jnp
from jax import lax
import numpy as np

D_MODEL = 4096
BATCH = 1
SEQ = 16384
DEPTH = 2
DEC_BATCH = 8
DEC_SEQ = 64
PAST_LEN = 4096

CHUNK = 64
N_MIXERS = 2
N_SSD_LAYERS = (DEPTH + 1) // 2
N_GDN_LAYERS = DEPTH // 2
CONV_W = 4
DEEPNORM_ALPHA = (2 * DEPTH) ** 0.25
DEEPNORM_BETA = (8 * DEPTH) ** -0.25
LN_EPS = 1e-5
RMS_EPS = 1e-5
L2_EPS = 1e-6

SSD_INNER = 2 * D_MODEL
SSD_HEAD_DIM = 64
SSD_HEADS = SSD_INNER // SSD_HEAD_DIM
SSD_GROUPS = 8
SSD_HEADS_PER_GROUP = SSD_HEADS // SSD_GROUPS
SSD_STATE = 128
SSD_CONV_DIM = SSD_INNER + 2 * SSD_GROUPS * SSD_STATE
SSD_PROJ = SSD_INNER + SSD_CONV_DIM + SSD_HEADS

GDN_HEAD_DIM = 128
GDN_K_HEADS = D_MODEL // GDN_HEAD_DIM
GDN_V_HEADS = 2 * GDN_K_HEADS
GDN_KEY_DIM = GDN_K_HEADS * GDN_HEAD_DIM
GDN_VAL_DIM = GDN_V_HEADS * GDN_HEAD_DIM
GDN_CONV_DIM = 2 * GDN_KEY_DIM + GDN_VAL_DIM
GDN_PROJ = GDN_CONV_DIM + GDN_VAL_DIM + 2 * GDN_V_HEADS

N_EXPERTS = 32
TOP_K = 4
MOE_D_FF = D_MODEL
SWIGLU_ALPHA = 1.702
SWIGLU_LIMIT = 7.0
MOE_BLOCK = 128

kernel_name = 'hybrid_ssd_gdn_moe_deepnorm_stream_step'


def layer_norm(x, g, b):
    xf = x.astype(jnp.float32)
    mu = jnp.mean(xf, axis=-1, keepdims=True)
    xc = xf - mu
    var = jnp.mean(xc * xc, axis=-1, keepdims=True)
    return (xc * lax.rsqrt(var + LN_EPS) * g + b).astype(x.dtype)


def rms_norm(x, eps):
    return x * lax.rsqrt(jnp.mean(x * x, axis=-1, keepdims=True) + eps)


def l2_norm(x):
    return x * lax.rsqrt(jnp.sum(x * x, axis=-1, keepdims=True) + L2_EPS)


def causal_conv(u, buf, w, b=None):
    L = u.shape[1]
    up = jnp.concatenate([buf.astype(u.dtype), u], axis=1)
    out = up[:, 0:L] * w[0]
    for j in range(1, CONV_W):
        out = out + up[:, j:j + L] * w[j]
    if b is not None:
        out = out + b
    return out, up[:, L:]


def to_chunks(t, q):
    bsz, L = t.shape[0], t.shape[1]
    return jnp.moveaxis(t.reshape((bsz, L // q, q) + t.shape[2:]), 1, 0)


def from_chunks(t):
    t = jnp.moveaxis(t, 0, 1)
    return t.reshape((t.shape[0], t.shape[1] * t.shape[2]) + t.shape[3:])


def ssd_mixer(x, ssm_state, conv_buf, w_in, conv_w, conv_b, dt_bias, a_log, d_skip, norm_g, w_out):
    bsz, L, _ = x.shape
    G, R, P, N = SSD_GROUPS, SSD_HEADS_PER_GROUP, SSD_HEAD_DIM, SSD_STATE
    proj = jnp.dot(x, w_in)
    z = proj[..., :SSD_INNER]
    xbc = proj[..., SSD_INNER:SSD_INNER + SSD_CONV_DIM]
    dt_raw = proj[..., SSD_INNER + SSD_CONV_DIM:]
    xbc, new_buf = causal_conv(xbc, conv_buf, conv_w, conv_b)
    xbc = jax.nn.silu(xbc).astype(jnp.float32)
    xs = xbc[..., :SSD_INNER].reshape(bsz, L, G, R, P)
    bm = xbc[..., SSD_INNER:SSD_INNER + G * N].reshape(bsz, L, G, N)
    cm = xbc[..., SSD_INNER + G * N:].reshape(bsz, L, G, N)
    dt = jax.nn.softplus(dt_raw.astype(jnp.float32) + dt_bias).reshape(bsz, L, G, R)
    a = -jnp.exp(a_log.astype(jnp.float32)).reshape(G, R)

    def step(h, inp):
        xc, dtc, bc, cc = inp
        n = xc.shape[1]
        cum = jnp.cumsum(dtc * a, axis=1)
        incl = jnp.tril(jnp.ones((n, n), dtype=bool))[None, :, :, None, None]
        decay = jnp.exp(jnp.where(incl, cum[:, :, None] - cum[:, None, :], -jnp.inf))
        xdt = xc * dtc[..., None]
        cb = jnp.einsum('btgn,bsgn->btsg', cc, bc)
        y = (jnp.einsum('btsg,btsgr,bsgrp->btgrp', cb, decay, xdt)
             + jnp.einsum('btgn,bgrpn->btgrp', cc, h) * jnp.exp(cum)[..., None])
        w_end = jnp.exp(cum[:, -1:] - cum)
        h = (h * jnp.exp(cum[:, -1])[..., None, None]
             + jnp.einsum('bsgn,bsgrp->bgrpn', bc, xdt * w_end[..., None]))
        return h, y

    q = min(CHUNK, L)
    h0 = ssm_state.astype(jnp.float32).reshape(bsz, G, R, P, N)
    h, y = lax.scan(step, h0, (to_chunks(xs, q), to_chunks(dt, q), to_chunks(bm, q), to_chunks(cm, q)))
    y = from_chunks(y) + xs * d_skip.astype(jnp.float32).reshape(G, R, 1)
    y = y.reshape(bsz, L, SSD_INNER) * jax.nn.silu(z.astype(jnp.float32))
    y = rms_norm(y.reshape(bsz, L, G, SSD_INNER // G), RMS_EPS).reshape(bsz, L, SSD_INNER) * norm_g
    out = jnp.dot(y.astype(x.dtype), w_out)
    return out, h.reshape(bsz, SSD_HEADS, P, N), new_buf


def gdn_chunk_step(S, inp):
    q, k, v, g, beta = inp
    rep = GDN_V_HEADS // GDN_K_HEADS
    q = jnp.swapaxes(jnp.repeat(q, rep, axis=2), 1, 2)
    k = jnp.swapaxes(jnp.repeat(k, rep, axis=2), 1, 2)
    v = jnp.swapaxes(v, 1, 2)
    g = jnp.swapaxes(g, 1, 2)
    beta = jnp.swapaxes(beta, 1, 2)
    n = q.shape[2]
    gc = jnp.cumsum(g, axis=-1)
    incl = jnp.tril(jnp.ones((n, n), dtype=bool))
    strict = jnp.tril(jnp.ones((n, n), dtype=bool), -1)
    decay = jnp.exp(jnp.where(incl, gc[..., :, None] - gc[..., None, :], -jnp.inf))
    kk = jnp.einsum('bhid,bhjd->bhij', k, k)
    a_mat = jnp.where(strict, kk * decay * beta[..., :, None], 0.0)
    rhs = jnp.concatenate([v * beta[..., None], k * (beta * jnp.exp(gc))[..., None]], axis=-1)
    sol = lax.linalg.triangular_solve(a_mat + jnp.eye(n, dtype=a_mat.dtype), rhs,
                                      left_side=True, lower=True, unit_diagonal=True)
    u, w = sol[..., :GDN_HEAD_DIM], sol[..., GDN_HEAD_DIM:]
    v_new = u - jnp.einsum('bhid,bhdv->bhiv', w, S)
    qk = jnp.einsum('bhid,bhjd->bhij', q, k) * decay
    o = (jnp.einsum('bhid,bhdv->bhiv', q * jnp.exp(gc)[..., None], S)
         + jnp.einsum('bhij,bhjv->bhiv', qk, v_new))
    g_last = gc[..., -1:]
    S = (S * jnp.exp(g_last)[..., None]
         + jnp.einsum('bhjd,bhjv->bhdv', k * jnp.exp(g_last - gc)[..., None], v_new))
    return S, jnp.swapaxes(o, 1, 2)


def gdn_mixer(x, S, conv_buf, w_in, conv_w, a_log, dt_bias, norm_g, w_out):
    bsz, L, _ = x.shape
    proj = jnp.dot(x, w_in)
    o_z = GDN_CONV_DIM
    o_b = o_z + GDN_VAL_DIM
    o_a = o_b + GDN_V_HEADS
    qkv, new_buf = causal_conv(proj[..., :o_z], conv_buf, conv_w)
    qkv = jax.nn.silu(qkv).astype(jnp.float32)
    z = proj[..., o_z:o_b].astype(jnp.float32).reshape(bsz, L, GDN_V_HEADS, GDN_HEAD_DIM)
    beta = jax.nn.sigmoid(proj[..., o_b:o_a].astype(jnp.float32))
    g = -jnp.exp(a_log.astype(jnp.float32)) * jax.nn.softplus(proj[..., o_a:].astype(jnp.float32) + dt_bias)
    q = l2_norm(qkv[..., :GDN_KEY_DIM].reshape(bsz, L, GDN_K_HEADS, GDN_HEAD_DIM)) * (GDN_HEAD_DIM ** -0.5)
    k = l2_norm(qkv[..., GDN_KEY_DIM:2 * GDN_KEY_DIM].reshape(bsz, L, GDN_K_HEADS, GDN_HEAD_DIM))
    v = qkv[..., 2 * GDN_KEY_DIM:].reshape(bsz, L, GDN_V_HEADS, GDN_HEAD_DIM)
    cq = min(CHUNK, L)
    S, o = lax.scan(gdn_chunk_step, S.astype(jnp.float32),
                    (to_chunks(q, cq), to_chunks(k, cq), to_chunks(v, cq), to_chunks(g, cq), to_chunks(beta, cq)))
    o = from_chunks(o)
    o = rms_norm(o, RMS_EPS) * norm_g * jax.nn.silu(z)
    out = jnp.dot(o.reshape(bsz, L, GDN_VAL_DIM).astype(x.dtype), w_out)
    return out, S, new_buf


def moe_ffn(x, layer, w_router, b_router, w_gate_up, b_gate_up, w_down, b_down):
    n_tok, d = x.shape
    n_assign = n_tok * TOP_K
    logits = jnp.dot(x, w_router[layer]).astype(jnp.float32) + b_router[layer]
    top_logit, top_e = lax.top_k(logits, TOP_K)
    gate = jax.nn.softmax(top_logit, axis=-1).reshape(-1)
    flat_e = top_e.reshape(-1)
    order = jnp.argsort(flat_e)
    e_sorted = flat_e[order]
    tok_sorted = (order // TOP_K).astype(jnp.int32)
    gate_sorted = gate[order]
    counts = jnp.bincount(flat_e, length=N_EXPERTS)
    padded = (counts + MOE_BLOCK - 1) // MOE_BLOCK * MOE_BLOCK
    pad_end = jnp.cumsum(padded)
    start = jnp.cumsum(counts) - counts
    dest = (pad_end - padded)[e_sorted] + jnp.arange(n_assign) - start[e_sorted]
    n_blocks = -(-n_assign // MOE_BLOCK) + N_EXPERTS
    row_tok = jnp.full((n_blocks * MOE_BLOCK,), n_tok, dtype=jnp.int32).at[dest].set(tok_sorted)
    block_e = jnp.minimum(jnp.searchsorted(pad_end, jnp.arange(n_blocks) * MOE_BLOCK, side='right'), N_EXPERTS - 1)
    x_pad = jnp.concatenate([x, jnp.zeros((1, d), x.dtype)], axis=0)
    x_rows = x_pad[row_tok].reshape(n_blocks, MOE_BLOCK, d)

    def expert_block(args):
        xb, e = args
        h = jnp.dot(xb, w_gate_up[layer, e]) + b_gate_up[layer, e]
        h_gate = jnp.minimum(h[..., :MOE_D_FF], SWIGLU_LIMIT)
        h_up = jnp.clip(h[..., MOE_D_FF:], -SWIGLU_LIMIT, SWIGLU_LIMIT)
        act = (h_up + 1.0) * h_gate * jax.nn.sigmoid(SWIGLU_ALPHA * h_gate)
        return jnp.dot(act, w_down[layer, e]) + b_down[layer, e]

    y_rows = lax.map(expert_block, (x_rows, block_e)).reshape(n_blocks * MOE_BLOCK, d)
    contrib = y_rows[dest].astype(jnp.float32) * gate_sorted[:, None]
    return jax.ops.segment_sum(contrib, tok_sorted, num_segments=n_tok)


def run_trunk(x, state_ssd, cache_ssd_conv, state_gdn, cache_gdn_conv, w):
    ssd_h, ssd_c, gdn_h, gdn_c = [], [], [], []
    for i in range(DEPTH):
        j = i // N_MIXERS
        if i % N_MIXERS == 0:
            mix, h, c = ssd_mixer(x, state_ssd[j], cache_ssd_conv[j], w['ssd_w_in'][j], w['ssd_conv_w'][j],
                                  w['ssd_conv_b'][j], w['ssd_dt_bias'][j], w['ssd_a_log'][j], w['ssd_d'][j],
                                  w['ssd_norm_g'][j], w['ssd_w_out'][j])
            ssd_h.append(h)
            ssd_c.append(c)
        else:
            mix, h, c = gdn_mixer(x, state_gdn[j], cache_gdn_conv[j], w['gdn_w_in'][j], w['gdn_conv_w'][j],
                                  w['gdn_a_log'][j], w['gdn_dt_bias'][j], w['gdn_norm_g'][j], w['gdn_w_out'][j])
            gdn_h.append(h)
            gdn_c.append(c)
        x = layer_norm(DEEPNORM_ALPHA * x + mix.astype(x.dtype), w['ln_g'][i, 0], w['ln_b'][i, 0])
        bsz, L, _ = x.shape
        ffn = moe_ffn(x.reshape(bsz * L, D_MODEL), i, w['moe_w_router'], w['moe_b_router'], w['moe_w_gate_up'],
                      w['moe_b_gate_up'], w['moe_w_down'], w['moe_b_down']).reshape(bsz, L, D_MODEL)
        x = layer_norm(DEEPNORM_ALPHA * x + ffn.astype(x.dtype), w['ln_g'][i, 1], w['ln_b'][i, 1])
    return x, jnp.stack(ssd_h), jnp.stack(ssd_c), jnp.stack(gdn_h), jnp.stack(gdn_c)


def _dt_bias(key, shape):
    dt = jnp.exp(jax.random.uniform(key, shape, jnp.float32, math.log(1e-3), math.log(1e-1)))
    return dt + jnp.log(-jnp.expm1(-dt))


def setup_inputs(seed: int = 0) -> dict:
    key = jax.random.key(seed)
    ks = jax.random.split(key, 32)
    f32 = jnp.float32

    def nrm(k, shape, scale=1.0):
        return jax.random.normal(k, shape, f32) * scale

    return {
        'x_prompt': nrm(ks[0], (BATCH, SEQ, D_MODEL)),
        'x_sample': nrm(ks[1], (DEC_BATCH, DEC_SEQ, D_MODEL)),
        'state_ssd': nrm(ks[2], (N_SSD_LAYERS, DEC_BATCH, SSD_HEADS, SSD_HEAD_DIM, SSD_STATE), 0.1),
        'cache_ssd_conv': nrm(ks[3], (N_SSD_LAYERS, DEC_BATCH, CONV_W - 1, SSD_CONV_DIM)),
        'state_gdn': nrm(ks[4], (N_GDN_LAYERS, DEC_BATCH, GDN_V_HEADS, GDN_HEAD_DIM, GDN_HEAD_DIM), 0.1),
        'cache_gdn_conv': nrm(ks[5], (N_GDN_LAYERS, DEC_BATCH, CONV_W - 1, GDN_CONV_DIM)),
        'ssd_w_in': nrm(ks[6], (N_SSD_LAYERS, D_MODEL, SSD_PROJ), D_MODEL ** -0.5),
        'ssd_conv_w': nrm(ks[7], (N_SSD_LAYERS, CONV_W, SSD_CONV_DIM), CONV_W ** -0.5),
        'ssd_conv_b': nrm(ks[8], (N_SSD_LAYERS, SSD_CONV_DIM), 0.01),
        'ssd_dt_bias': _dt_bias(ks[9], (N_SSD_LAYERS, SSD_HEADS)),
        'ssd_a_log': jnp.log(jax.random.uniform(ks[10], (N_SSD_LAYERS, SSD_HEADS), f32, 1.0, 16.0)),
        'ssd_d': 1.0 + nrm(ks[11], (N_SSD_LAYERS, SSD_HEADS), 0.01),
        'ssd_norm_g': 1.0 + nrm(ks[12], (N_SSD_LAYERS, SSD_INNER), 0.01),
        'ssd_w_out': nrm(ks[13], (N_SSD_LAYERS, SSD_INNER, D_MODEL), SSD_INNER ** -0.5 * DEEPNORM_BETA),
        'gdn_w_in': nrm(ks[14], (N_GDN_LAYERS, D_MODEL, GDN_PROJ), D_MODEL ** -0.5),
        'gdn_conv_w': nrm(ks[15], (N_GDN_LAYERS, CONV_W, GDN_CONV_DIM), CONV_W ** -0.5),
        'gdn_a_log': jnp.log(jax.random.uniform(ks[16], (N_GDN_LAYERS, GDN_V_HEADS), f32, 1.0, 16.0)),
        'gdn_dt_bias': _dt_bias(ks[17], (N_GDN_LAYERS, GDN_V_HEADS)),
        'gdn_norm_g': 1.0 + nrm(ks[18], (N_GDN_LAYERS, GDN_HEAD_DIM), 0.01),
        'gdn_w_out': nrm(ks[19], (N_GDN_LAYERS, GDN_VAL_DIM, D_MODEL), GDN_VAL_DIM ** -0.5 * DEEPNORM_BETA),
        'moe_w_router': nrm(ks[20], (DEPTH, D_MODEL, N_EXPERTS), D_MODEL ** -0.5),
        'moe_b_router': nrm(ks[21], (DEPTH, N_EXPERTS), 0.01),
        'moe_w_gate_up': nrm(ks[22], (DEPTH, N_EXPERTS, D_MODEL, 2 * MOE_D_FF), D_MODEL ** -0.5),
        'moe_b_gate_up': nrm(ks[23], (DEPTH, N_EXPERTS, 2 * MOE_D_FF), 0.01),
        'moe_w_down': nrm(ks[24], (DEPTH, N_EXPERTS, MOE_D_FF, D_MODEL), MOE_D_FF ** -0.5 * DEEPNORM_BETA),
        'moe_b_down': nrm(ks[25], (DEPTH, N_EXPERTS, D_MODEL), 0.01),
        'ln_g': 1.0 + nrm(ks[26], (DEPTH, 2, D_MODEL), 0.01),
        'ln_b': nrm(ks[27], (DEPTH, 2, D_MODEL), 0.01),
    }


def reference(x_prompt, x_sample, state_ssd, cache_ssd_conv, state_gdn, cache_gdn_conv,
              ssd_w_in, ssd_conv_w, ssd_conv_b, ssd_dt_bias, ssd_a_log, ssd_d, ssd_norm_g, ssd_w_out,
              gdn_w_in, gdn_conv_w, gdn_a_log, gdn_dt_bias, gdn_norm_g, gdn_w_out,
              moe_w_router, moe_b_router, moe_w_gate_up, moe_b_gate_up, moe_w_down, moe_b_down,
              ln_g, ln_b):
    w = {
        'ssd_w_in': ssd_w_in, 'ssd_conv_w': ssd_conv_w, 'ssd_conv_b': ssd_conv_b, 'ssd_dt_bias': ssd_dt_bias,
        'ssd_a_log': ssd_a_log, 'ssd_d': ssd_d, 'ssd_norm_g': ssd_norm_g, 'ssd_w_out': ssd_w_out,
        'gdn_w_in': gdn_w_in, 'gdn_conv_w': gdn_conv_w, 'gdn_a_log': gdn_a_log, 'gdn_dt_bias': gdn_dt_bias,
        'gdn_norm_g': gdn_norm_g, 'gdn_w_out': gdn_w_out,
        'moe_w_router': moe_w_router, 'moe_b_router': moe_b_router, 'moe_w_gate_up': moe_w_gate_up,
        'moe_b_gate_up': moe_b_gate_up, 'moe_w_down': moe_w_down, 'moe_b_down': moe_b_down,
        'ln_g': ln_g, 'ln_b': ln_b,
    }
    bp = x_prompt.shape[0]
    zero_ssd = jnp.zeros((N_SSD_LAYERS, bp, SSD_HEADS, SSD_HEAD_DIM, SSD_STATE), jnp.float32)
    zero_ssd_conv = jnp.zeros((N_SSD_LAYERS, bp, CONV_W - 1, SSD_CONV_DIM), x_prompt.dtype)
    zero_gdn = jnp.zeros((N_GDN_LAYERS, bp, GDN_V_HEADS, GDN_HEAD_DIM, GDN_HEAD_DIM), jnp.float32)
    zero_gdn_conv = jnp.zeros((N_GDN_LAYERS, bp, CONV_W - 1, GDN_CONV_DIM), x_prompt.dtype)
    y_prompt, ssd_h_p, ssd_c_p, gdn_h_p, gdn_c_p = run_trunk(
        x_prompt, zero_ssd, zero_ssd_conv, zero_gdn, zero_gdn_conv, w)
    y_sample, ssd_h_s, ssd_c_s, gdn_h_s, gdn_c_s = run_trunk(
        x_sample, state_ssd, cache_ssd_conv, state_gdn, cache_gdn_conv, w)
    return (y_prompt, y_sample, ssd_h_p, ssd_c_p, gdn_h_p, gdn_c_p, ssd_h_s, ssd_c_s, gdn_h_s, gdn_c_s)
```

```python
import functools
import math

import jax
import jax.numpy as jnp
from jax import lax
from jax.experimental import pallas as pl
from jax.experimental.pallas import tpu as pltpu

F32 = jnp.float32
BF16 = jnp.bfloat16

CONV_W = 4
LN_EPS = 1e-5
RMS_EPS = 1e-5
L2_EPS = 1e-6
SSD_HEAD_DIM = 64
SSD_GROUPS = 8
SSD_STATE = 128
GDN_HEAD_DIM = 128
GDN_CHUNK = 64
TOP_K = 4
SWIGLU_ALPHA = 1.702
SWIGLU_LIMIT = 7.0

LANES = 128
SUBLANES = 8
VMEM_CAP_BYTES = 60000 * 1024
CONV_TAIL = SUBLANES


def _tile(dim, pref, align):
    t = min(pref, dim) // align * align
    while t >= align:
        if dim % t == 0:
            return t
        t -= align
    return dim


def _params(sem, vmem_bytes):
    limit = int(min(VMEM_CAP_BYTES, max(32 * 1024 * 1024, vmem_bytes * 5 // 4 + (4 << 20))))
    return pltpu.CompilerParams(dimension_semantics=sem, vmem_limit_bytes=limit)


def _silu(v):
    return v * jax.nn.sigmoid(v)


def _softplus(v):
    return jnp.maximum(v, 0.0) + jnp.log1p(jnp.exp(-jnp.abs(v)))


def _dot(a, b):
    return jnp.dot(a.astype(BF16), b.astype(BF16), preferred_element_type=F32)


def _dot_nt(a, b):
    return lax.dot_general(a.astype(BF16), b.astype(BF16), (((1,), (1,)), ((), ())),
                           preferred_element_type=F32)


def _dot_f32(a, b):
    return jnp.dot(a, b, preferred_element_type=F32, precision=lax.Precision.HIGHEST)


def _mm_kernel(x_ref, w_ref, o_ref, wb_ref, *, k_rows, k_step):
    @pl.when(pl.program_id(1) == 0)
    def _():
        def body(c, carry):
            r = pl.multiple_of(c * k_step, k_step)
            wb_ref[pl.ds(r, k_step), :] = w_ref[pl.ds(r, k_step), :].astype(BF16)
            return carry
        lax.fori_loop(0, k_rows // k_step, body, 0)

    o_ref[...] = jnp.dot(x_ref[...], wb_ref[...], preferred_element_type=F32)


def _matmul(x, w, col0, ncols, tm_pref=512, tn_pref=512):
    m, k = x.shape
    tm = _tile(m, tm_pref, 16)
    tn = _tile(math.gcd(ncols, col0) if col0 else ncols, tn_pref, LANES)
    if tn % LANES and not (col0 == 0 and ncols == w.shape[1]):
        raise ValueError("projection column range must be lane aligned")
    k_step = _tile(k, 256, 16)
    j0 = col0 // tn
    vmem = tm * k * 2 * 2 + k * tn * 4 * 2 + k * tn * 2 + tm * tn * 4 * 2 + tm * tn * 4
    return pl.pallas_call(
        functools.partial(_mm_kernel, k_rows=k, k_step=k_step),
        grid=(ncols // tn, m // tm),
        in_specs=[pl.BlockSpec((tm, k), lambda j, i: (i, 0)),
                  pl.BlockSpec((k, tn), lambda j, i: (0, j + j0))],
        out_specs=pl.BlockSpec((tm, tn), lambda j, i: (i, j)),
        out_shape=jax.ShapeDtypeStruct((m, ncols), F32),
        scratch_shapes=[pltpu.VMEM((k, tn), BF16)],
        compiler_params=_params(("arbitrary", "arbitrary"), vmem),
    )(x, w)


def _ln_rows(v, g, b):
    mu = jnp.mean(v, axis=-1, keepdims=True)
    vc = v - mu
    var = jnp.mean(vc * vc, axis=-1, keepdims=True)
    return vc * lax.rsqrt(var + LN_EPS) * g + b


def _ln_kernel(x_ref, y_ref, g_ref, b_ref, o_ref, ob_ref, *, alpha):
    out = _ln_rows(alpha * x_ref[...] + y_ref[...], g_ref[...], b_ref[...])
    o_ref[...] = out
    ob_ref[...] = out.astype(BF16)


def _deepnorm_ln(x, y, g, b, alpha):
    t, d = x.shape
    tr = _tile(t, 256, 16)
    row = pl.BlockSpec((tr, d), lambda i: (i, 0))
    vec = pl.BlockSpec((1, d), lambda i: (0, 0))
    return pl.pallas_call(
        functools.partial(_ln_kernel, alpha=alpha),
        grid=(t // tr,),
        in_specs=[row, row, vec, vec],
        out_specs=[row, row],
        out_shape=[jax.ShapeDtypeStruct((t, d), F32), jax.ShapeDtypeStruct((t, d), BF16)],
        compiler_params=_params(("parallel",), tr * d * 4 * 8),
    )(x, y, g.reshape(1, d), b.reshape(1, d))


def _tri_incl(q):
    r = lax.broadcasted_iota(jnp.int32, (q, q), 0)
    c = lax.broadcasted_iota(jnp.int32, (q, q), 1)
    return (r >= c).astype(F32)


def _ssd_prep_kernel(dt_ref, bias_ref, alog_ref, dto_ref, cum_ref, *, q):
    dt = _softplus(dt_ref[...] + bias_ref[...])
    dto_ref[...] = dt
    cum_ref[...] = _dot_f32(_tri_incl(q), dt * (-jnp.exp(alog_ref[...])))


def _ssd_prep(dt_raw, dt_bias, a_log, q, row0, n_rows):
    h = dt_raw.shape[1]
    c0 = row0 // q
    row = pl.BlockSpec((q, h), lambda i: (i, 0))
    vec = pl.BlockSpec((1, h), lambda i: (0, 0))
    return pl.pallas_call(
        functools.partial(_ssd_prep_kernel, q=q),
        grid=(n_rows // q,),
        in_specs=[pl.BlockSpec((q, h), lambda i: (c0 + i, 0)), vec, vec],
        out_specs=[row, row],
        out_shape=[jax.ShapeDtypeStruct((n_rows, h), F32)] * 2,
        compiler_params=_params(("parallel",), 1 << 20),
    )(dt_raw, dt_bias.reshape(1, h), a_log.reshape(1, h))


def _gdn_prep_kernel(ba_ref, alog_ref, bias_ref, beta_ref, gc_ref, *, q, hv):
    ba = ba_ref[...]
    beta_ref[...] = jax.nn.sigmoid(ba[:, :hv])
    g = -jnp.exp(alog_ref[...]) * _softplus(ba[:, hv:] + bias_ref[...])
    gc_ref[...] = _dot_f32(_tri_incl(q), g)


def _gdn_prep(ba, a_log, dt_bias, q):
    t, hv2 = ba.shape
    hv = hv2 // 2
    vec = pl.BlockSpec((1, hv), lambda i: (0, 0))
    out = pl.BlockSpec((q, hv), lambda i: (i, 0))
    return pl.pallas_call(
        functools.partial(_gdn_prep_kernel, q=q, hv=hv),
        grid=(t // q,),
        in_specs=[pl.BlockSpec((q, hv2), lambda i: (i, 0)), vec, vec],
        out_specs=[out, out],
        out_shape=[jax.ShapeDtypeStruct((t, hv), F32)] * 2,
        compiler_params=_params(("parallel",), 1 << 20),
    )(ba, a_log.reshape(1, hv), dt_bias.reshape(1, hv))


def _conv_init(cat_ref, buf_ref):
    cat_ref[0:CONV_TAIL, :] = jnp.zeros((CONV_TAIL, cat_ref.shape[1]), F32)
    cat_ref[CONV_TAIL - (CONV_W - 1):CONV_TAIL, :] = buf_ref[0]


def _conv_window(cat_ref, u, w_ref, rows):
    cat_ref[CONV_TAIL:CONV_TAIL + rows, :] = u
    base = CONV_TAIL - (CONV_W - 1)
    out = cat_ref[base:base + rows, :] * w_ref[0:1, :]
    for j in range(1, CONV_W):
        out = out + cat_ref[base + j:base + j + rows, :] * w_ref[j:j + 1, :]
    cat_ref[0:CONV_TAIL, :] = cat_ref[rows:rows + CONV_TAIL, :]
    return out


def _ssd_kernel(cumc_ref, dtc_ref, cumr_ref, z_ref, x_ref, b_ref, c_ref,
                wx_ref, wb_ref, wc_ref, bx_ref, bb_ref, bc_ref, ux_ref, ub_ref, uc_ref,
                d_ref, ng_ref, h0_ref,
                y_ref, hout_ref,
                hs_ref, catx_ref, catb_ref, catc_ref, ybuf_ref, *, q, n_pairs, n_chunks):
    c_idx = pl.program_id(2)
    p2 = 2 * SSD_HEAD_DIM

    @pl.when(c_idx == 0)
    def _():
        hs_ref[...] = h0_ref[0]
        _conv_init(catx_ref, ux_ref)
        _conv_init(catb_ref, ub_ref)
        _conv_init(catc_ref, uc_ref)

    xs = _silu(_conv_window(catx_ref, x_ref[...], wx_ref, q) + bx_ref[...])
    bm = _silu(_conv_window(catb_ref, b_ref[...], wb_ref, q) + bb_ref[...])
    cm = _silu(_conv_window(catc_ref, c_ref[...], wc_ref, q) + bc_ref[...])

    cumc = cumc_ref[0, 0]
    dtc = dtc_ref[0, 0]
    cumr = cumr_ref[0, 0]
    cb = _dot_nt(cm, bm)
    rr = lax.broadcasted_iota(jnp.int32, (q, q), 0)
    cc = lax.broadcasted_iota(jnp.int32, (q, q), 1)
    tril = rr >= cc
    lo = lax.broadcasted_iota(jnp.int32, (q, p2), 1) < SSD_HEAD_DIM
    lo_row = lax.broadcasted_iota(jnp.int32, (1, p2), 1) < SSD_HEAD_DIM
    top = lax.broadcasted_iota(jnp.int32, (p2, 1), 0) < SSD_HEAD_DIM
    cm_b = cm.astype(BF16)
    bm_b = bm.astype(BF16)

    for pair in range(n_pairs):
        ra, rb = 2 * pair, 2 * pair + 1
        ca, cbk = cumc[:, ra:ra + 1], cumc[:, rb:rb + 1]
        cum_pair = jnp.where(lo, ca, cbk)
        dt_pair = jnp.where(lo, dtc[:, ra:ra + 1], dtc[:, rb:rb + 1])
        last_a = cumr[ra:ra + 1, q - 1:q]
        last_b = cumr[rb:rb + 1, q - 1:q]
        last_pair = jnp.where(lo_row, last_a, last_b)
        xdt = xs[:, pair * p2:(pair + 1) * p2] * dt_pair
        xw = xdt * jnp.exp(last_pair - cum_pair)
        m_a = cb * jnp.exp(jnp.where(tril, ca - cumr[ra:ra + 1, :], -jnp.inf))
        m_b = cb * jnp.exp(jnp.where(tril, cbk - cumr[rb:rb + 1, :], -jnp.inf))
        y = _dot(m_a, jnp.where(lo, xdt, 0.0)) + _dot(m_b, jnp.where(lo, 0.0, xdt))
        hp = hs_ref[ra:ra + 2].reshape(p2, SSD_STATE)
        y = y + _dot_nt(cm_b, hp) * jnp.exp(cum_pair)
        ybuf_ref[:, pair * p2:(pair + 1) * p2] = y
        scale = jnp.where(top, jnp.exp(last_a), jnp.exp(last_b))
        hn = hp * scale + _dot(xw.T, bm_b)
        hs_ref[ra:ra + 2] = hn.reshape(2, SSD_HEAD_DIM, SSD_STATE)

    yf = (ybuf_ref[...] + xs * d_ref[...]) * _silu(z_ref[...])
    ms = jnp.mean(yf * yf, axis=-1, keepdims=True)
    y_ref[...] = (yf * lax.rsqrt(ms + RMS_EPS) * ng_ref[...]).astype(BF16)

    @pl.when(c_idx == n_chunks - 1)
    def _():
        hout_ref[0] = hs_ref[...]


def _ssd_scan(proj, cumc, dtc, cumr, conv_w, conv_b, conv_buf, d_exp, norm_g, h0,
              *, row0, bsz, seq, q, inner):
    g_n, n = SSD_GROUPS, SSD_STATE
    gw = inner // g_n
    r = gw // SSD_HEAD_DIM
    nc = seq // q
    c0 = row0 // q
    xb = inner // gw
    bb = 2 * inner // n
    cvb = inner // n

    def rowblk(s, g, c):
        return c0 + s * nc + c

    in_specs = [
        pl.BlockSpec((1, 1, q, r), lambda s, g, c: (g, s * nc + c, 0, 0)),
        pl.BlockSpec((1, 1, q, r), lambda s, g, c: (g, s * nc + c, 0, 0)),
        pl.BlockSpec((1, 1, r, q), lambda s, g, c: (g, s * nc + c, 0, 0)),
        pl.BlockSpec((q, gw), lambda s, g, c: (rowblk(s, g, c), g)),
        pl.BlockSpec((q, gw), lambda s, g, c: (rowblk(s, g, c), xb + g)),
        pl.BlockSpec((q, n), lambda s, g, c: (rowblk(s, g, c), bb + g)),
        pl.BlockSpec((q, n), lambda s, g, c: (rowblk(s, g, c), bb + g_n + g)),
        pl.BlockSpec((CONV_W, gw), lambda s, g, c: (0, g)),
        pl.BlockSpec((CONV_W, n), lambda s, g, c: (0, cvb + g)),
        pl.BlockSpec((CONV_W, n), lambda s, g, c: (0, cvb + g_n + g)),
        pl.BlockSpec((1, gw), lambda s, g, c: (0, g)),
        pl.BlockSpec((1, n), lambda s, g, c: (0, cvb + g)),
        pl.BlockSpec((1, n), lambda s, g, c: (0, cvb + g_n + g)),
        pl.BlockSpec((1, CONV_W - 1, gw), lambda s, g, c: (s, 0, g)),
        pl.BlockSpec((1, CONV_W - 1, n), lambda s, g, c: (s, 0, cvb + g)),
        pl.BlockSpec((1, CONV_W - 1, n), lambda s, g, c: (s, 0, cvb + g_n + g)),
        pl.BlockSpec((1, gw), lambda s, g, c: (0, g)),
        pl.BlockSpec((1, gw), lambda s, g, c: (0, g)),
        pl.BlockSpec((1, r, SSD_HEAD_DIM, n), lambda s, g, c: (s, g, 0, 0)),
    ]
    out_specs = [
        pl.BlockSpec((q, gw), lambda s, g, c: (s * nc + c, g)),
        pl.BlockSpec((1, r, SSD_HEAD_DIM, n), lambda s, g, c: (s, g, 0, 0)),
    ]
    out_shape = [jax.ShapeDtypeStruct((bsz * seq, inner), BF16),
                 jax.ShapeDtypeStruct(h0.shape, F32)]
    scratch = [pltpu.VMEM((r, SSD_HEAD_DIM, n), F32),
               pltpu.VMEM((q + CONV_TAIL, gw), F32),
               pltpu.VMEM((q + CONV_TAIL, n), F32),
               pltpu.VMEM((q + CONV_TAIL, n), F32),
               pltpu.VMEM((q, gw), F32)]
    vmem = q * gw * 4 * 12 + q * q * 4 * 8
    cb2 = conv_b.reshape(1, -1)
    return pl.pallas_call(
        functools.partial(_ssd_kernel, q=q, n_pairs=r // 2, n_chunks=nc),
        grid=(bsz, g_n, nc),
        in_specs=in_specs, out_specs=out_specs, out_shape=out_shape, scratch_shapes=scratch,
        compiler_params=_params(("arbitrary", "arbitrary", "arbitrary"), vmem),
    )(cumc, dtc, cumr, proj, proj, proj, proj, conv_w, conv_w, conv_w, cb2, cb2, cb2,
      conv_buf, conv_buf, conv_buf, d_exp, norm_g.reshape(1, -1), h0)


def _gdn_kernel(col_ref, row_ref, q_ref, k_ref, v_ref, z_ref, wq_ref, wk_ref, wv_ref,
                uq_ref, uk_ref, uv_ref, ng_ref, s0_ref,
                o_ref, sout_ref,
                s_ref, catq_ref, catk_ref, catv_ref, *, cps, n_steps, rep):
    c = GDN_CHUNK
    dh = GDN_HEAD_DIM
    rows = cps * c
    step = pl.program_id(2)

    @pl.when(step == 0)
    def _():
        s_ref[...] = s0_ref[0]
        _conv_init(catq_ref, uq_ref)
        _conv_init(catk_ref, uk_ref)
        _conv_init(catv_ref, uv_ref)

    qa = _silu(_conv_window(catq_ref, q_ref[...], wq_ref, rows))
    ka = _silu(_conv_window(catk_ref, k_ref[...], wk_ref, rows))
    va = _silu(_conv_window(catv_ref, v_ref[...], wv_ref, rows))
    qa = qa * lax.rsqrt(jnp.sum(qa * qa, axis=-1, keepdims=True) + L2_EPS) * (dh ** -0.5)
    ka = ka * lax.rsqrt(jnp.sum(ka * ka, axis=-1, keepdims=True) + L2_EPS)

    nb = cps * rep
    m = nb * c
    k_big = jnp.concatenate([ka[(i // rep) * c:(i // rep + 1) * c] for i in range(nb)], axis=0)
    q_big = jnp.concatenate([qa[(i // rep) * c:(i // rep + 1) * c] for i in range(nb)], axis=0)
    v_big = jnp.concatenate(
        [va[(i // rep) * c:(i // rep + 1) * c, (i % rep) * dh:(i % rep + 1) * dh] for i in range(nb)],
        axis=0)
    gcol = jnp.concatenate(
        [col_ref[0, i // rep][:, (i % rep):(i % rep) + 1] for i in range(nb)], axis=0)
    bcol = jnp.concatenate(
        [col_ref[0, i // rep][:, rep + (i % rep):rep + (i % rep) + 1] for i in range(nb)], axis=0)
    grow = jnp.concatenate(
        [row_ref[0, i // rep][(i % rep):(i % rep) + 1, :] for i in range(nb)], axis=1)
    glast = jnp.concatenate(
        [jnp.broadcast_to(row_ref[0, i // rep][(i % rep):(i % rep) + 1, c - 1:c], (c, 1))
         for i in range(nb)], axis=0)

    ri = lax.broadcasted_iota(jnp.int32, (m, m), 0)
    ci = lax.broadcasted_iota(jnp.int32, (m, m), 1)
    shift = c.bit_length() - 1
    same = jnp.right_shift(ri, shift) == jnp.right_shift(ci, shift)
    incl = same & (ri >= ci)
    strict = same & (ri > ci)
    decay = jnp.exp(jnp.where(incl, gcol - grow, -jnp.inf))
    kk = _dot_nt(k_big, k_big)
    a_mat = jnp.where(strict, kk * decay * bcol, 0.0)
    eye = (ri == ci).astype(F32)
    t_inv = eye - a_mat
    pw = a_mat
    steps = max(1, int(math.ceil(math.log2(c))) - 1)
    for _ in range(steps):
        pw = _dot(pw, pw)
        t_inv = t_inv + _dot(t_inv, pw)
    egc = jnp.exp(gcol)
    rhs = jnp.concatenate([v_big * bcol, k_big * (bcol * egc)], axis=1)
    sol = _dot(t_inv, rhs)
    u_big, w_big = sol[:, :dh], sol[:, dh:]
    qk = _dot_nt(q_big, k_big) * decay
    qe = q_big * egc
    kd = k_big * jnp.exp(glast - gcol)
    eg_last = jnp.exp(glast)

    for ch in range(cps):
        outs = []
        for h in range(rep):
            i = ch * rep + h
            sl = slice(i * c, (i + 1) * c)
            s_h = s_ref[h]
            wq_s = _dot(jnp.concatenate([w_big[sl], qe[sl]], axis=0), s_h)
            v_new = u_big[sl] - wq_s[:c]
            o = wq_s[c:] + _dot(qk[sl, sl], v_new)
            s_ref[h] = s_h * eg_last[i * c:i * c + 1, :] + _dot(kd[sl].T, v_new)
            o = o * lax.rsqrt(jnp.mean(o * o, axis=-1, keepdims=True) + RMS_EPS) * ng_ref[...]
            outs.append(o)
        zc = z_ref[ch * c:(ch + 1) * c, :]
        o_ref[ch * c:(ch + 1) * c, :] = (jnp.concatenate(outs, axis=1) * _silu(zc)).astype(BF16)

    @pl.when(step == n_steps - 1)
    def _():
        sout_ref[0] = s_ref[...]


def _gdn_scan(proj, colf, rowf, conv_w, conv_buf, norm_g, s0, *, row0, bsz, seq, cps, key_dim, val_dim):
    dh, c = GDN_HEAD_DIM, GDN_CHUNK
    hk = key_dim // dh
    rep = val_dim // key_dim
    vw = rep * dh
    rows = cps * c
    n_steps = seq // rows
    r0 = row0 // rows
    c0 = row0 // c
    kb = key_dim // dh
    vb = 2 * key_dim // vw
    zb = (2 * key_dim + val_dim) // vw

    def rowblk(s, h, t):
        return r0 + s * n_steps + t

    in_specs = [
        pl.BlockSpec((1, cps, c, 2 * rep), lambda s, h, t: (h, (c0 // cps) + s * n_steps + t, 0, 0)),
        pl.BlockSpec((1, cps, 2 * rep, c), lambda s, h, t: (h, (c0 // cps) + s * n_steps + t, 0, 0)),
        pl.BlockSpec((rows, dh), lambda s, h, t: (rowblk(s, h, t), h)),
        pl.BlockSpec((rows, dh), lambda s, h, t: (rowblk(s, h, t), kb + h)),
        pl.BlockSpec((rows, vw), lambda s, h, t: (rowblk(s, h, t), vb + h)),
        pl.BlockSpec((rows, vw), lambda s, h, t: (rowblk(s, h, t), zb + h)),
        pl.BlockSpec((CONV_W, dh), lambda s, h, t: (0, h)),
        pl.BlockSpec((CONV_W, dh), lambda s, h, t: (0, kb + h)),
        pl.BlockSpec((CONV_W, vw), lambda s, h, t: (0, vb + h)),
        pl.BlockSpec((1, CONV_W - 1, dh), lambda s, h, t: (s, 0, h)),
        pl.BlockSpec((1, CONV_W - 1, dh), lambda s, h, t: (s, 0, kb + h)),
        pl.BlockSpec((1, CONV_W - 1, vw), lambda s, h, t: (s, 0, vb + h)),
        pl.BlockSpec((1, dh), lambda s, h, t: (0, 0)),
        pl.BlockSpec((1, rep, dh, dh), lambda s, h, t: (s, h, 0, 0)),
    ]
    out_specs = [
        pl.BlockSpec((rows, vw), lambda s, h, t: (s * n_steps + t, h)),
        pl.BlockSpec((1, rep, dh, dh), lambda s, h, t: (s, h, 0, 0)),
    ]
    out_shape = [jax.ShapeDtypeStruct((bsz * seq, val_dim), BF16),
                 jax.ShapeDtypeStruct(s0.shape, F32)]
    scratch = [pltpu.VMEM((rep, dh, dh), F32),
               pltpu.VMEM((rows + CONV_TAIL, dh), F32),
               pltpu.VMEM((rows + CONV_TAIL, dh), F32),
               pltpu.VMEM((rows + CONV_TAIL, vw), F32)]
    m = cps * rep * c
    vmem = rows * (2 * dh + 2 * vw) * 4 * 6 + m * m * 4 * 12
    return pl.pallas_call(
        functools.partial(_gdn_kernel, cps=cps, n_steps=n_steps, rep=rep),
        grid=(bsz, hk, n_steps),
        in_specs=in_specs, out_specs=out_specs, out_shape=out_shape, scratch_shapes=scratch,
        compiler_params=_params(("arbitrary", "arbitrary", "arbitrary"), vmem),
    )(colf, rowf, proj, proj, proj, proj, conv_w, conv_w, conv_w,
      conv_buf, conv_buf, conv_buf, norm_g.reshape(1, dh), s0)


def _split_bf16(a):
    hi = a.astype(BF16)
    lo = (a - hi.astype(F32)).astype(BF16)
    return hi, lo


def _router_kernel(x_ref, wt_ref, b_ref, e_ref, gate_ref, rank_ref, cnt_ref, carry_ref, *, n_steps):
    i = pl.program_id(0)
    n_e = wt_ref.shape[0]
    tr = x_ref.shape[0]

    @pl.when(i == 0)
    def _():
        carry_ref[...] = jnp.zeros_like(carry_ref)

    xh, xl = _split_bf16(x_ref[...])
    wh, wl = _split_bf16(wt_ref[...])
    nt = (((1,), (1,)), ((), ()))
    logits = (lax.dot_general(wh, xh, nt, preferred_element_type=F32)
              + lax.dot_general(wh, xl, nt, preferred_element_type=F32)
              + lax.dot_general(wl, xh, nt, preferred_element_type=F32)) + b_ref[...]

    eidx = lax.broadcasted_iota(jnp.int32, (n_e, tr), 0)
    work = logits
    tops, idxs = [], []
    for _ in range(TOP_K):
        mval = jnp.max(work, axis=0, keepdims=True)
        idx = jnp.min(jnp.where(work == mval, eidx, n_e), axis=0, keepdims=True)
        tops.append(mval)
        idxs.append(idx)
        work = jnp.where(eidx == idx, -jnp.inf, work)
    exps = [jnp.exp(t - tops[0]) for t in tops]
    den = exps[0]
    for e in exps[1:]:
        den = den + e

    onehot = jnp.zeros((n_e, tr), F32)
    for idx in idxs:
        onehot = onehot + (eidx == idx).astype(F32)
    rr = lax.broadcasted_iota(jnp.int32, (tr, tr), 0)
    cc = lax.broadcasted_iota(jnp.int32, (tr, tr), 1)
    upper = (rr <= cc).astype(BF16)
    incl = jnp.dot(onehot.astype(BF16), upper, preferred_element_type=F32)
    before = carry_ref[...][:, 0:1] + incl - onehot
    for k in range(TOP_K):
        e_ref[k:k + 1, :] = idxs[k]
        gate_ref[k:k + 1, :] = exps[k] / den
        rank = jnp.sum(jnp.where(eidx == idxs[k], before, 0.0), axis=0, keepdims=True)
        rank_ref[k:k + 1, :] = rank.astype(jnp.int32)
    carry_ref[...] = carry_ref[...] + incl[:, tr - 1:tr]

    @pl.when(i == n_steps - 1)
    def _():
        cnt_ref[...] = carry_ref[...].astype(jnp.int32)


def _router(x, w_router, b_router):
    t, d = x.shape
    n_e = w_router.shape[1]
    tr = _tile(t, 512, LANES)
    n_steps = t // tr
    tok = pl.BlockSpec((TOP_K, tr), lambda i: (0, i))
    return pl.pallas_call(
        functools.partial(_router_kernel, n_steps=n_steps),
        grid=(n_steps,),
        in_specs=[pl.BlockSpec((tr, d), lambda i: (i, 0)),
                  pl.BlockSpec((n_e, d), lambda i: (0, 0)),
                  pl.BlockSpec((n_e, 1), lambda i: (0, 0))],
        out_specs=[tok, tok, tok, pl.BlockSpec((n_e, LANES), lambda i: (0, 0))],
        out_shape=[jax.ShapeDtypeStruct((TOP_K, t), jnp.int32),
                   jax.ShapeDtypeStruct((TOP_K, t), F32),
                   jax.ShapeDtypeStruct((TOP_K, t), jnp.int32),
                   jax.ShapeDtypeStruct((n_e, LANES), jnp.int32)],
        scratch_shapes=[pltpu.VMEM((n_e, LANES), F32)],
        compiler_params=_params(("arbitrary",), tr * d * 4 * 4 + tr * tr * 8),
    )(x, w_router.T, b_router.reshape(n_e, 1))


def _dispatch_kernel(pos_ref, x_ref, init_ref, xs_ref, sem, *, tt):
    del init_ref
    t0 = pl.program_id(0) * tt

    def issue(t, carry):
        for k in range(TOP_K):
            dst = pos_ref[(t0 + t) * TOP_K + k]
            pltpu.make_async_copy(x_ref.at[pl.ds(t0 + t, 1)], xs_ref.at[pl.ds(dst, 1)], sem).start()
        return carry
    lax.fori_loop(0, tt, issue, 0)

    def drain(t, carry):
        for k in range(TOP_K):
            pltpu.make_async_copy(x_ref.at[pl.ds(0, 1)], xs_ref.at[pl.ds(0, 1)], sem).wait()
        return carry
    lax.fori_loop(0, tt, drain, 0)


def _dispatch(x, pos_flat, n_rows):
    t, d = x.shape
    tt = _tile(t, 128, 1)
    grid_spec = pltpu.PrefetchScalarGridSpec(
        num_scalar_prefetch=1,
        grid=(t // tt,),
        in_specs=[pl.BlockSpec(memory_space=pl.ANY), pl.BlockSpec(memory_space=pl.ANY)],
        out_specs=pl.BlockSpec(memory_space=pl.ANY),
        scratch_shapes=[pltpu.SemaphoreType.DMA(())],
    )
    return pl.pallas_call(
        functools.partial(_dispatch_kernel, tt=tt),
        grid_spec=grid_spec,
        out_shape=jax.ShapeDtypeStruct((n_rows, d), x.dtype),
        input_output_aliases={2: 0},
        compiler_params=pltpu.CompilerParams(dimension_semantics=("arbitrary",)),
    )(pos_flat, x, jnp.zeros((n_rows, d), x.dtype))


def _cast_rows(dst_ref, src_ref, k_rows, k_step):
    def body(c, carry):
        r = pl.multiple_of(c * k_step, k_step)
        dst_ref[pl.ds(r, k_step), :] = src_ref[0, pl.ds(r, k_step), :].astype(BF16)
        return carry
    lax.fori_loop(0, k_rows // k_step, body, 0)


def _is_new_expert(be_ref, blk):
    prev = be_ref[jnp.maximum(blk - 1, 0)]
    return (blk == 0) | (be_ref[blk] != prev)


def _gate_up_kernel(be_ref, nv_ref, x_ref, wg_ref, wu_ref, bg_ref, bu_ref, o_ref, wgb_ref, wub_ref,
                    *, k_rows, k_step):
    blk = pl.program_id(1)

    @pl.when(_is_new_expert(be_ref, blk))
    def _():
        _cast_rows(wgb_ref, wg_ref, k_rows, k_step)
        _cast_rows(wub_ref, wu_ref, k_rows, k_step)

    @pl.when(blk < nv_ref[0])
    def _():
        xb = x_ref[...].astype(BF16)
        hg = jnp.dot(xb, wgb_ref[...], preferred_element_type=F32) + bg_ref[0]
        hu = jnp.dot(xb, wub_ref[...], preferred_element_type=F32) + bu_ref[0]
        hg = jnp.minimum(hg, SWIGLU_LIMIT)
        hu = jnp.clip(hu, -SWIGLU_LIMIT, SWIGLU_LIMIT)
        o_ref[...] = ((hu + 1.0) * hg * jax.nn.sigmoid(SWIGLU_ALPHA * hg)).astype(BF16)

    @pl.when(blk >= nv_ref[0])
    def _():
        o_ref[...] = jnp.zeros_like(o_ref)


def _moe_gate_up(xs, w_gate_up, b_gate_up, layer, block_e, n_valid, tm, tn_pref=512):
    n_rows, d = xs.shape
    n_e = w_gate_up.shape[1]
    f = w_gate_up.shape[3] // 2
    tn = _tile(f, tn_pref, LANES)
    nj = f // tn
    k_step = _tile(d, 256, 16)
    w4 = w_gate_up
    b4 = b_gate_up.reshape(b_gate_up.shape[0] * n_e, 1, 2 * f)
    w3 = w4.reshape(w4.shape[0] * n_e, d, 2 * f)
    e0 = layer * n_e
    grid_spec = pltpu.PrefetchScalarGridSpec(
        num_scalar_prefetch=2,
        grid=(nj, n_rows // tm),
        in_specs=[
            pl.BlockSpec((tm, d), lambda j, b, be, nv: (b, 0)),
            pl.BlockSpec((1, d, tn), lambda j, b, be, nv: (e0 + be[b], 0, j)),
            pl.BlockSpec((1, d, tn), lambda j, b, be, nv: (e0 + be[b], 0, nj + j)),
            pl.BlockSpec((1, 1, tn), lambda j, b, be, nv: (e0 + be[b], 0, j)),
            pl.BlockSpec((1, 1, tn), lambda j, b, be, nv: (e0 + be[b], 0, nj + j)),
        ],
        out_specs=pl.BlockSpec((tm, tn), lambda j, b, be, nv: (b, j)),
        scratch_shapes=[pltpu.VMEM((d, tn), BF16), pltpu.VMEM((d, tn), BF16)],
    )
    vmem = tm * d * 4 * 2 + d * tn * 4 * 4 + d * tn * 2 * 2 + tm * tn * 2 * 2 + tm * tn * 4 * 4 + tm * d * 2
    return pl.pallas_call(
        functools.partial(_gate_up_kernel, k_rows=d, k_step=k_step),
        grid_spec=grid_spec,
        out_shape=jax.ShapeDtypeStruct((n_rows, f), BF16),
        compiler_params=_params(("arbitrary", "arbitrary"), vmem),
    )(block_e, n_valid, xs, w3, w3, b4, b4)


def _down_kernel(be_ref, nv_ref, a_ref, w_ref, b_ref, o_ref, wb_ref, *, k_rows, k_step):
    blk = pl.program_id(1)

    @pl.when(_is_new_expert(be_ref, blk))
    def _():
        _cast_rows(wb_ref, w_ref, k_rows, k_step)

    @pl.when(blk < nv_ref[0])
    def _():
        o_ref[...] = jnp.dot(a_ref[...], wb_ref[...], preferred_element_type=F32) + b_ref[0]

    @pl.when(blk >= nv_ref[0])
    def _():
        o_ref[...] = jnp.zeros_like(o_ref)


def _moe_down(act, w_down, b_down, layer, block_e, n_valid, tm, tn_pref=512):
    n_rows, f = act.shape
    n_e = w_down.shape[1]
    d = w_down.shape[3]
    tn = _tile(d, tn_pref, LANES)
    k_step = _tile(f, 256, 16)
    w3 = w_down.reshape(w_down.shape[0] * n_e, f, d)
    b3 = b_down.reshape(b_down.shape[0] * n_e, 1, d)
    e0 = layer * n_e
    grid_spec = pltpu.PrefetchScalarGridSpec(
        num_scalar_prefetch=2,
        grid=(d // tn, n_rows // tm),
        in_specs=[
            pl.BlockSpec((tm, f), lambda j, b, be, nv: (b, 0)),
            pl.BlockSpec((1, f, tn), lambda j, b, be, nv: (e0 + be[b], 0, j)),
            pl.BlockSpec((1, 1, tn), lambda j, b, be, nv: (e0 + be[b], 0, j)),
        ],
        out_specs=pl.BlockSpec((tm, tn), lambda j, b, be, nv: (b, j)),
        scratch_shapes=[pltpu.VMEM((f, tn), BF16)],
    )
    vmem = tm * f * 2 * 2 + f * tn * 4 * 2 + f * tn * 2 + tm * tn * 4 * 3
    return pl.pallas_call(
        functools.partial(_down_kernel, k_rows=f, k_step=k_step),
        grid_spec=grid_spec,
        out_shape=jax.ShapeDtypeStruct((n_rows, d), F32),
        compiler_params=_params(("arbitrary", "arbitrary"), vmem),
    )(block_e, n_valid, act, w3, b3)


def _combine_kernel(pos_ref, y_ref, x_ref, gate_ref, g_ref, b_ref, o_ref, ob_ref, buf_ref, sem,
                    *, tt, n_steps, alpha):
    i = pl.program_id(0)
    slot = i % 2

    def row_copy(step, t, k, s):
        src = pos_ref[(step * tt + t) * TOP_K + k]
        return pltpu.make_async_copy(y_ref.at[pl.ds(src, 1)], buf_ref.at[s, k, pl.ds(t, 1)], sem.at[s])

    def issue(step, s):
        def body(t, carry):
            for k in range(TOP_K):
                row_copy(step, t, k, s).start()
            return carry
        lax.fori_loop(0, tt, body, 0)

    @pl.when(i == 0)
    def _():
        issue(0, 0)

    @pl.when(i + 1 < n_steps)
    def _():
        issue(i + 1, 1 - slot)

    def drain(t, carry):
        for k in range(TOP_K):
            pltpu.make_async_copy(y_ref.at[pl.ds(0, 1)], buf_ref.at[slot, k, pl.ds(t, 1)], sem.at[slot]).wait()
        return carry
    lax.fori_loop(0, tt, drain, 0)

    gates = gate_ref[...]
    acc = alpha * x_ref[...]
    for k in range(TOP_K):
        acc = acc + buf_ref[slot, k] * gates[:, k:k + 1]
    out = _ln_rows(acc, g_ref[...], b_ref[...])
    o_ref[...] = out
    ob_ref[...] = out.astype(BF16)


def _moe_combine_ln(y_rows, pos_flat, gates_t, x, g, b, alpha):
    t, d = x.shape
    tt = _tile(t, 64, 16)
    n_steps = t // tt
    row = pl.BlockSpec((tt, d), lambda i, pos: (i, 0))
    vec = pl.BlockSpec((1, d), lambda i, pos: (0, 0))
    grid_spec = pltpu.PrefetchScalarGridSpec(
        num_scalar_prefetch=1,
        grid=(n_steps,),
        in_specs=[pl.BlockSpec(memory_space=pl.ANY), row,
                  pl.BlockSpec((tt, TOP_K), lambda i, pos: (i, 0)), vec, vec],
        out_specs=[row, row],
        scratch_shapes=[pltpu.VMEM((2, TOP_K, tt, d), F32), pltpu.SemaphoreType.DMA((2,))],
    )
    vmem = 2 * TOP_K * tt * d * 4 + tt * d * 4 * 8
    return pl.pallas_call(
        functools.partial(_combine_kernel, tt=tt, n_steps=n_steps, alpha=alpha),
        grid_spec=grid_spec,
        out_shape=[jax.ShapeDtypeStruct((t, d), F32), jax.ShapeDtypeStruct((t, d), BF16)],
        compiler_params=_params(("arbitrary",), vmem),
    )(pos_flat, y_rows, x, gates_t, g.reshape(1, d), b.reshape(1, d))


def _moe_layer(x, layer, w_router, b_router, w_gate_up, b_gate_up, w_down, b_down, ln_g, ln_b, alpha,
               tm_pref=256):
    t, d = x.shape
    n_e = w_router.shape[2]
    tm = _tile(t, tm_pref, 16)
    top_e, gates, rank, cnt = _router(x, w_router[layer], b_router[layer])
    counts = cnt[:, 0]
    padded = (counts + tm - 1) // tm * tm
    pad_end = jnp.cumsum(padded)
    pad_start = pad_end - padded
    n_blocks = (t * TOP_K) // tm + n_e
    n_rows = n_blocks * tm
    pos = (pad_start[top_e] + rank).T.reshape(-1).astype(jnp.int32)
    blk_row0 = jnp.arange(n_blocks, dtype=jnp.int32) * tm
    n_valid = (pad_end[-1] // tm).astype(jnp.int32).reshape(1)
    block_e = jnp.searchsorted(pad_end, jnp.minimum(blk_row0, pad_end[-1] - 1), side='right')
    block_e = jnp.minimum(block_e, n_e - 1).astype(jnp.int32)

    xs = _dispatch(x, pos, n_rows)
    act = _moe_gate_up(xs, w_gate_up, b_gate_up, layer, block_e, n_valid, tm)
    y_rows = _moe_down(act, w_down, b_down, layer, block_e, n_valid, tm)
    return _moe_combine_ln(y_rows, pos, gates.T, x, ln_g[layer, 1], ln_b[layer, 1], alpha)


def _last_rows(proj, row0, bsz, seq, col0, col1):
    ends = [row0 + (s + 1) * seq for s in range(bsz)]
    return jnp.stack([proj[e - (CONV_W - 1):e, col0:col1] for e in ends])


def _per_head_forms(col_arrays, n_groups, q):
    t = col_arrays[0].shape[0]
    parts = [a.reshape(t // q, q, n_groups, -1) for a in col_arrays]
    colf = jnp.concatenate(parts, axis=-1).transpose(2, 0, 1, 3)
    return colf, colf.transpose(0, 1, 3, 2)


def _ssd_layer(xb, groups, state, conv_cache, w_in, conv_w, conv_b, dt_bias, a_log, d_skip, norm_g, w_out):
    inner = w_out.shape[0]
    heads = inner // SSD_HEAD_DIM
    gn = SSD_GROUPS * SSD_STATE
    main_cols = 2 * inner + 2 * gn
    proj = _matmul(xb, w_in, 0, main_cols)
    dt_raw = _matmul(xb, w_in[:, main_cols:], 0, heads)
    d_exp = jnp.repeat(d_skip, SSD_HEAD_DIM).reshape(1, inner)
    ys, hs, caches = [], [], []
    for (row0, bsz, seq, q), h0, cbuf in zip(groups, state, conv_cache):
        dt, cum = _ssd_prep(dt_raw, dt_bias, a_log, q, row0, bsz * seq)
        colf, rowf = _per_head_forms([cum, dt], SSD_GROUPS, q)
        r = heads // SSD_GROUPS
        y, h = _ssd_scan(proj, colf[..., :r], colf[..., r:], rowf[:, :, :r], conv_w, conv_b, cbuf,
                         d_exp, norm_g, h0, row0=row0, bsz=bsz, seq=seq, q=q, inner=inner)
        ys.append(y)
        hs.append(h)
        caches.append(_last_rows(proj, row0, bsz, seq, inner, 2 * inner + 2 * gn))
    mix = _matmul(jnp.concatenate(ys, axis=0), w_out, 0, w_out.shape[1], tn_pref=256)
    return mix, hs, caches


def _gdn_layer(xb, groups, state, conv_cache, w_in, conv_w, a_log, dt_bias, norm_g, w_out):
    val_dim = w_out.shape[0]
    hv = a_log.shape[0]
    key_dim = (conv_w.shape[1] - val_dim) // 2
    hk = key_dim // GDN_HEAD_DIM
    main_cols = 2 * key_dim + 2 * val_dim
    proj = _matmul(xb, w_in, 0, main_cols)
    ba = _matmul(xb, w_in[:, main_cols:], 0, 2 * hv)
    beta, gc = _gdn_prep(ba, a_log, dt_bias, GDN_CHUNK)
    colf, rowf = _per_head_forms([gc, beta], hk, GDN_CHUNK)
    os_, ss, caches = [], [], []
    for (row0, bsz, seq, cps), s0, cbuf in zip(groups, state, conv_cache):
        o, s = _gdn_scan(proj, colf, rowf, conv_w, cbuf, norm_g, s0, row0=row0, bsz=bsz, seq=seq,
                         cps=cps, key_dim=key_dim, val_dim=val_dim)
        os_.append(o)
        ss.append(s)
        caches.append(_last_rows(proj, row0, bsz, seq, 0, 2 * key_dim + val_dim))
    mix = _matmul(jnp.concatenate(os_, axis=0), w_out, 0, w_out.shape[1], tn_pref=256)
    return mix, ss, caches


def kernel(x_prompt, x_sample, state_ssd, cache_ssd_conv, state_gdn, cache_gdn_conv, ssd_w_in, ssd_conv_w, ssd_conv_b, ssd_dt_bias, ssd_a_log, ssd_d, ssd_norm_g, ssd_w_out, gdn_w_in, gdn_conv_w, gdn_a_log, gdn_dt_bias, gdn_norm_g, gdn_w_out, moe_w_router, moe_b_router, moe_w_gate_up, moe_b_gate_up, moe_w_down, moe_b_down, ln_g, ln_b):
    bp, lp, d = x_prompt.shape
    bs, ls, _ = x_sample.shape
    depth = ln_g.shape[0]
    alpha = (2 * depth) ** 0.25
    tp = bp * lp
    x = jnp.concatenate([x_prompt.reshape(tp, d), x_sample.reshape(bs * ls, d)], axis=0)
    xb = x.astype(BF16)

    ssd_q_prompt = _tile(lp, 256, GDN_CHUNK)
    ssd_q_sample = _tile(ls, 256, GDN_CHUNK)
    if tp % ssd_q_sample or tp % ssd_q_prompt:
        raise ValueError("prompt rows must be chunk aligned")
    gdn_cps_prompt = _tile(lp // GDN_CHUNK, 2, 1)
    gdn_cps_sample = _tile(ls // GDN_CHUNK, 2, 1)
    ssd_groups = [(0, bp, lp, ssd_q_prompt), (tp, bs, ls, ssd_q_sample)]
    gdn_groups = [(0, bp, lp, gdn_cps_prompt), (tp, bs, ls, gdn_cps_sample)]

    ssd_h, ssd_c, gdn_h, gdn_c = [], [], [], []
    for i in range(depth):
        j = i // 2
        if i % 2 == 0:
            zero_h = jnp.zeros((bp,) + state_ssd.shape[2:], F32)
            zero_c = jnp.zeros((bp,) + cache_ssd_conv.shape[2:], F32)
            mix, hs, cs = _ssd_layer(xb, ssd_groups, [zero_h, state_ssd[j]], [zero_c, cache_ssd_conv[j]],
                                     ssd_w_in[j], ssd_conv_w[j], ssd_conv_b[j], ssd_dt_bias[j], ssd_a_log[j],
                                     ssd_d[j], ssd_norm_g[j], ssd_w_out[j])
            ssd_h.append(hs)
            ssd_c.append(cs)
        else:
            zero_h = jnp.zeros((bp,) + state_gdn.shape[2:], F32)
            zero_c = jnp.zeros((bp,) + cache_gdn_conv.shape[2:], F32)
            mix, hs, cs = _gdn_layer(xb, gdn_groups, [zero_h, state_gdn[j]], [zero_c, cache_gdn_conv[j]],
                                     gdn_w_in[j], gdn_conv_w[j], gdn_a_log[j], gdn_dt_bias[j], gdn_norm_g[j],
                                     gdn_w_out[j])
            gdn_h.append(hs)
            gdn_c.append(cs)
        x, xb = _deepnorm_ln(x, mix, ln_g[i, 0], ln_b[i, 0], alpha)
        x, xb = _moe_layer(x, i, moe_w_router, moe_b_router, moe_w_gate_up, moe_b_gate_up, moe_w_down,
                           moe_b_down, ln_g, ln_b, alpha)

    def stack(items, which):
        return jnp.stack([it[which] for it in items])

    y_prompt = x[:tp].reshape(bp, lp, d)
    y_sample = x[tp:].reshape(bs, ls, d)
    return (y_prompt, y_sample,
            stack(ssd_h, 0), stack(ssd_c, 0), stack(gdn_h, 0), stack(gdn_c, 0),
            stack(ssd_h, 1), stack(ssd_c, 1), stack(gdn_h, 1), stack(gdn_c, 1))
```

```python
import functools
import math

import jax
import jax.numpy as jnp
from jax import lax
from jax.experimental import pallas as pl
from jax.experimental.pallas import tpu as pltpu

F32 = jnp.float32
BF16 = jnp.bfloat16

CONV_W = 4
LN_EPS = 1e-5
RMS_EPS = 1e-5
L2_EPS = 1e-6
SSD_HEAD_DIM = 64
SSD_GROUPS = 8
SSD_STATE = 128
GDN_HEAD_DIM = 128
GDN_CHUNK = 64
GDN_HEADS_PER_PROGRAM = 4
TOP_K = 4
SWIGLU_ALPHA = 1.702
SWIGLU_LIMIT = 7.0

LANES = 128
SUBLANES = 8
VMEM_CAP_BYTES = 60000 * 1024
CONV_TAIL = SUBLANES


def _tile(dim, pref, align):
    t = min(pref, dim) // align * align
    while t >= align:
        if dim % t == 0:
            return t
        t -= align
    return dim


def _params(sem, vmem_bytes):
    limit = int(min(VMEM_CAP_BYTES, max(32 * 1024 * 1024, vmem_bytes * 5 // 4 + (4 << 20))))
    return pltpu.CompilerParams(dimension_semantics=sem, vmem_limit_bytes=limit)


def _silu(v):
    return v * jax.nn.sigmoid(v)


def _softplus(v):
    return jnp.maximum(v, 0.0) + jnp.log1p(jnp.exp(-jnp.abs(v)))


def _dot(a, b):
    return jnp.dot(a.astype(BF16), b.astype(BF16), preferred_element_type=F32)


def _dot_nt(a, b):
    return lax.dot_general(a.astype(BF16), b.astype(BF16), (((1,), (1,)), ((), ())),
                           preferred_element_type=F32)


def _dot_f32(a, b):
    return jnp.dot(a, b, preferred_element_type=F32, precision=lax.Precision.HIGHEST)


def _mm_kernel(x_ref, w_ref, o_ref, wb_ref, *, k_rows, k_step):
    @pl.when(pl.program_id(1) == 0)
    def _():
        def body(c, carry):
            r = pl.multiple_of(c * k_step, k_step)
            wb_ref[pl.ds(r, k_step), :] = w_ref[pl.ds(r, k_step), :].astype(BF16)
            return carry
        lax.fori_loop(0, k_rows // k_step, body, 0)

    o_ref[...] = jnp.dot(x_ref[...], wb_ref[...], preferred_element_type=F32)


def _matmul(x, w, col0, ncols, tm_pref=512, tn_pref=512):
    m, k = x.shape
    tm = _tile(m, tm_pref, 16)
    tn = _tile(math.gcd(ncols, col0) if col0 else ncols, tn_pref, LANES)
    if tn % LANES and not (col0 == 0 and ncols == w.shape[1]):
        raise ValueError("projection column range must be lane aligned")
    k_step = _tile(k, 256, 16)
    j0 = col0 // tn
    vmem = tm * k * 2 * 2 + k * tn * 4 * 2 + k * tn * 2 + tm * tn * 4 * 2 + tm * tn * 4
    return pl.pallas_call(
        functools.partial(_mm_kernel, k_rows=k, k_step=k_step),
        grid=(ncols // tn, m // tm),
        in_specs=[pl.BlockSpec((tm, k), lambda j, i: (i, 0)),
                  pl.BlockSpec((k, tn), lambda j, i: (0, j + j0))],
        out_specs=pl.BlockSpec((tm, tn), lambda j, i: (i, j)),
        out_shape=jax.ShapeDtypeStruct((m, ncols), F32),
        scratch_shapes=[pltpu.VMEM((k, tn), BF16)],
        compiler_params=_params(("arbitrary", "arbitrary"), vmem),
    )(x, w)


def _ln_rows(v, g, b):
    mu = jnp.mean(v, axis=-1, keepdims=True)
    vc = v - mu
    var = jnp.mean(vc * vc, axis=-1, keepdims=True)
    return vc * lax.rsqrt(var + LN_EPS) * g + b


def _ln_kernel(x_ref, y_ref, g_ref, b_ref, o_ref, ob_ref, *, alpha):
    out = _ln_rows(alpha * x_ref[...] + y_ref[...], g_ref[...], b_ref[...])
    o_ref[...] = out
    ob_ref[...] = out.astype(BF16)


def _deepnorm_ln(x, y, g, b, alpha):
    t, d = x.shape
    tr = _tile(t, 256, 16)
    row = pl.BlockSpec((tr, d), lambda i: (i, 0))
    vec = pl.BlockSpec((1, d), lambda i: (0, 0))
    return pl.pallas_call(
        functools.partial(_ln_kernel, alpha=alpha),
        grid=(t // tr,),
        in_specs=[row, row, vec, vec],
        out_specs=[row, row],
        out_shape=[jax.ShapeDtypeStruct((t, d), F32), jax.ShapeDtypeStruct((t, d), BF16)],
        compiler_params=_params(("parallel",), tr * d * 4 * 8),
    )(x, y, g.reshape(1, d), b.reshape(1, d))


def _tri_incl(q):
    r = lax.broadcasted_iota(jnp.int32, (q, q), 0)
    c = lax.broadcasted_iota(jnp.int32, (q, q), 1)
    return (r >= c).astype(F32)


def _ssd_prep_kernel(dt_ref, bias_ref, alog_ref, dto_ref, cum_ref, *, q):
    dt = _softplus(dt_ref[...] + bias_ref[...])
    dto_ref[...] = dt
    cum_ref[...] = _dot_f32(_tri_incl(q), dt * (-jnp.exp(alog_ref[...])))


def _ssd_prep(dt_raw, dt_bias, a_log, q, row0, n_rows):
    h = dt_raw.shape[1]
    c0 = row0 // q
    row = pl.BlockSpec((q, h), lambda i: (i, 0))
    vec = pl.BlockSpec((1, h), lambda i: (0, 0))
    return pl.pallas_call(
        functools.partial(_ssd_prep_kernel, q=q),
        grid=(n_rows // q,),
        in_specs=[pl.BlockSpec((q, h), lambda i: (c0 + i, 0)), vec, vec],
        out_specs=[row, row],
        out_shape=[jax.ShapeDtypeStruct((n_rows, h), F32)] * 2,
        compiler_params=_params(("parallel",), 1 << 20),
    )(dt_raw, dt_bias.reshape(1, h), a_log.reshape(1, h))


def _gdn_prep_kernel(ba_ref, alog_ref, bias_ref, beta_ref, gc_ref, *, q, hv):
    ba = ba_ref[...]
    beta_ref[...] = jax.nn.sigmoid(ba[:, :hv])
    g = -jnp.exp(alog_ref[...]) * _softplus(ba[:, hv:] + bias_ref[...])
    gc_ref[...] = _dot_f32(_tri_incl(q), g)


def _gdn_prep(ba, a_log, dt_bias, q):
    t, hv2 = ba.shape
    hv = hv2 // 2
    vec = pl.BlockSpec((1, hv), lambda i: (0, 0))
    out = pl.BlockSpec((q, hv), lambda i: (i, 0))
    return pl.pallas_call(
        functools.partial(_gdn_prep_kernel, q=q, hv=hv),
        grid=(t // q,),
        in_specs=[pl.BlockSpec((q, hv2), lambda i: (i, 0)), vec, vec],
        out_specs=[out, out],
        out_shape=[jax.ShapeDtypeStruct((t, hv), F32)] * 2,
        compiler_params=_params(("parallel",), 1 << 20),
    )(ba, a_log.reshape(1, hv), dt_bias.reshape(1, hv))


def _conv_init(cat_ref, buf_ref):
    cat_ref[0:CONV_TAIL, :] = jnp.zeros((CONV_TAIL, cat_ref.shape[1]), F32)
    cat_ref[CONV_TAIL - (CONV_W - 1):CONV_TAIL, :] = buf_ref[0]


def _conv_window(cat_ref, u, w_ref, rows):
    cat_ref[CONV_TAIL:CONV_TAIL + rows, :] = u
    base = CONV_TAIL - (CONV_W - 1)
    out = cat_ref[base:base + rows, :] * w_ref[0:1, :]
    for j in range(1, CONV_W):
        out = out + cat_ref[base + j:base + j + rows, :] * w_ref[j:j + 1, :]
    cat_ref[0:CONV_TAIL, :] = cat_ref[rows:rows + CONV_TAIL, :]
    return out


def _ssd_kernel(cumc_ref, dtc_ref, cumr_ref, z_ref, x_ref, b_ref, c_ref,
                wx_ref, wb_ref, wc_ref, bx_ref, bb_ref, bc_ref, ux_ref, ub_ref, uc_ref,
                d_ref, ng_ref, h0_ref,
                y_ref, hout_ref,
                hs_ref, catx_ref, catb_ref, catc_ref, ybuf_ref, *, q, n_pairs, n_chunks):
    c_idx = pl.program_id(2)
    p2 = 2 * SSD_HEAD_DIM

    @pl.when(c_idx == 0)
    def _():
        hs_ref[...] = h0_ref[0]
        _conv_init(catx_ref, ux_ref)
        _conv_init(catb_ref, ub_ref)
        _conv_init(catc_ref, uc_ref)

    xs = _silu(_conv_window(catx_ref, x_ref[...], wx_ref, q) + bx_ref[...])
    bm = _silu(_conv_window(catb_ref, b_ref[...], wb_ref, q) + bb_ref[...])
    cm = _silu(_conv_window(catc_ref, c_ref[...], wc_ref, q) + bc_ref[...])

    cumc = cumc_ref[0, 0]
    dtc = dtc_ref[0, 0]
    cumr = cumr_ref[0, 0]
    cb = _dot_nt(cm, bm)
    rr = lax.broadcasted_iota(jnp.int32, (q, q), 0)
    cc = lax.broadcasted_iota(jnp.int32, (q, q), 1)
    tril = rr >= cc
    lo = lax.broadcasted_iota(jnp.int32, (q, p2), 1) < SSD_HEAD_DIM
    lo_row = lax.broadcasted_iota(jnp.int32, (1, p2), 1) < SSD_HEAD_DIM
    top = lax.broadcasted_iota(jnp.int32, (p2, 1), 0) < SSD_HEAD_DIM
    cm_b = cm.astype(BF16)
    bm_b = bm.astype(BF16)

    for pair in range(n_pairs):
        ra, rb = 2 * pair, 2 * pair + 1
        ca, cbk = cumc[:, ra:ra + 1], cumc[:, rb:rb + 1]
        cum_pair = jnp.where(lo, ca, cbk)
        dt_pair = jnp.where(lo, dtc[:, ra:ra + 1], dtc[:, rb:rb + 1])
        last_a = cumr[ra:ra + 1, q - 1:q]
        last_b = cumr[rb:rb + 1, q - 1:q]
        last_pair = jnp.where(lo_row, last_a, last_b)
        xdt = xs[:, pair * p2:(pair + 1) * p2] * dt_pair
        xw = xdt * jnp.exp(last_pair - cum_pair)
        m_a = cb * jnp.exp(jnp.where(tril, ca - cumr[ra:ra + 1, :], -jnp.inf))
        m_b = cb * jnp.exp(jnp.where(tril, cbk - cumr[rb:rb + 1, :], -jnp.inf))
        y = _dot(m_a, jnp.where(lo, xdt, 0.0)) + _dot(m_b, jnp.where(lo, 0.0, xdt))
        hp = hs_ref[ra:ra + 2].reshape(p2, SSD_STATE)
        y = y + _dot_nt(cm_b, hp) * jnp.exp(cum_pair)
        ybuf_ref[:, pair * p2:(pair + 1) * p2] = y
        scale = jnp.where(top, jnp.exp(last_a), jnp.exp(last_b))
        hn = hp * scale + _dot(xw.T, bm_b)
        hs_ref[ra:ra + 2] = hn.reshape(2, SSD_HEAD_DIM, SSD_STATE)

    yf = (ybuf_ref[...] + xs * d_ref[...]) * _silu(z_ref[...])
    ms = jnp.mean(yf * yf, axis=-1, keepdims=True)
    y_ref[...] = (yf * lax.rsqrt(ms + RMS_EPS) * ng_ref[...]).astype(BF16)

    @pl.when(c_idx == n_chunks - 1)
    def _():
        hout_ref[0] = hs_ref[...]


def _ssd_scan(proj, cumc, dtc, cumr, conv_w, conv_b, conv_buf, d_exp, norm_g, h0,
              *, row0, bsz, seq, q, inner):
    g_n, n = SSD_GROUPS, SSD_STATE
    gw = inner // g_n
    r = gw // SSD_HEAD_DIM
    nc = seq // q
    c0 = row0 // q
    xb = inner // gw
    bb = 2 * inner // n
    cvb = inner // n

    def rowblk(s, g, c):
        return c0 + s * nc + c

    in_specs = [
        pl.BlockSpec((1, 1, q, r), lambda s, g, c: (g, s * nc + c, 0, 0)),
        pl.BlockSpec((1, 1, q, r), lambda s, g, c: (g, s * nc + c, 0, 0)),
        pl.BlockSpec((1, 1, r, q), lambda s, g, c: (g, s * nc + c, 0, 0)),
        pl.BlockSpec((q, gw), lambda s, g, c: (rowblk(s, g, c), g)),
        pl.BlockSpec((q, gw), lambda s, g, c: (rowblk(s, g, c), xb + g)),
        pl.BlockSpec((q, n), lambda s, g, c: (rowblk(s, g, c), bb + g)),
        pl.BlockSpec((q, n), lambda s, g, c: (rowblk(s, g, c), bb + g_n + g)),
        pl.BlockSpec((CONV_W, gw), lambda s, g, c: (0, g)),
        pl.BlockSpec((CONV_W, n), lambda s, g, c: (0, cvb + g)),
        pl.BlockSpec((CONV_W, n), lambda s, g, c: (0, cvb + g_n + g)),
        pl.BlockSpec((1, gw), lambda s, g, c: (0, g)),
        pl.BlockSpec((1, n), lambda s, g, c: (0, cvb + g)),
        pl.BlockSpec((1, n), lambda s, g, c: (0, cvb + g_n + g)),
        pl.BlockSpec((1, CONV_W - 1, gw), lambda s, g, c: (s, 0, g)),
        pl.BlockSpec((1, CONV_W - 1, n), lambda s, g, c: (s, 0, cvb + g)),
        pl.BlockSpec((1, CONV_W - 1, n), lambda s, g, c: (s, 0, cvb + g_n + g)),
        pl.BlockSpec((1, gw), lambda s, g, c: (0, g)),
        pl.BlockSpec((1, gw), lambda s, g, c: (0, g)),
        pl.BlockSpec((1, r, SSD_HEAD_DIM, n), lambda s, g, c: (s, g, 0, 0)),
    ]
    out_specs = [
        pl.BlockSpec((q, gw), lambda s, g, c: (s * nc + c, g)),
        pl.BlockSpec((1, r, SSD_HEAD_DIM, n), lambda s, g, c: (s, g, 0, 0)),
    ]
    out_shape = [jax.ShapeDtypeStruct((bsz * seq, inner), BF16),
                 jax.ShapeDtypeStruct(h0.shape, F32)]
    scratch = [pltpu.VMEM((r, SSD_HEAD_DIM, n), F32),
               pltpu.VMEM((q + CONV_TAIL, gw), F32),
               pltpu.VMEM((q + CONV_TAIL, n), F32),
               pltpu.VMEM((q + CONV_TAIL, n), F32),
               pltpu.VMEM((q, gw), F32)]
    vmem = q * gw * 4 * 12 + q * q * 4 * 8
    cb2 = conv_b.reshape(1, -1)
    return pl.pallas_call(
        functools.partial(_ssd_kernel, q=q, n_pairs=r // 2, n_chunks=nc),
        grid=(bsz, g_n, nc),
        in_specs=in_specs, out_specs=out_specs, out_shape=out_shape, scratch_shapes=scratch,
        compiler_params=_params(("arbitrary", "arbitrary", "arbitrary"), vmem),
    )(cumc, dtc, cumr, proj, proj, proj, proj, conv_w, conv_w, conv_w, cb2, cb2, cb2,
      conv_buf, conv_buf, conv_buf, d_exp, norm_g.reshape(1, -1), h0)


def _gdn_kernel(col_ref, row_ref, q_ref, k_ref, v_ref, z_ref, wq_ref, wk_ref, wv_ref,
                uq_ref, uk_ref, uv_ref, ng_ref, s0_ref,
                o_ref, sout_ref,
                s_ref, catq_ref, catk_ref, catv_ref, *, cps, n_steps, rep, hpb):
    c = GDN_CHUNK
    dh = GDN_HEAD_DIM
    vw = rep * dh
    rows = cps * c
    step = pl.program_id(2)

    @pl.when(step == 0)
    def _():
        s_ref[...] = s0_ref[0]
        _conv_init(catq_ref, uq_ref)
        _conv_init(catk_ref, uk_ref)
        _conv_init(catv_ref, uv_ref)

    q_all = _silu(_conv_window(catq_ref, q_ref[...], wq_ref, rows))
    k_all = _silu(_conv_window(catk_ref, k_ref[...], wk_ref, rows))
    v_all = _silu(_conv_window(catv_ref, v_ref[...], wv_ref, rows))

    nb = cps * rep
    m = nb * c
    ri = lax.broadcasted_iota(jnp.int32, (m, m), 0)
    ci = lax.broadcasted_iota(jnp.int32, (m, m), 1)
    shift = c.bit_length() - 1
    same = jnp.right_shift(ri, shift) == jnp.right_shift(ci, shift)
    incl = same & (ri >= ci)
    strict = same & (ri > ci)
    eye = (ri == ci).astype(F32)
    steps = max(1, int(math.ceil(math.log2(c))) - 1)

    s_all = [s_ref[i] for i in range(hpb * rep)]
    heads = range(hpb)
    ng = ng_ref[...]

    def l2n(a):
        return a * lax.rsqrt(jnp.sum(a * a, axis=-1, keepdims=True) + L2_EPS)

    qa = [l2n(q_all[:, kh * dh:(kh + 1) * dh]) * (dh ** -0.5) for kh in heads]
    ka = [l2n(k_all[:, kh * dh:(kh + 1) * dh]) for kh in heads]
    cols = [[col_ref[kh, ch] for ch in range(cps)] for kh in heads]
    rws = [[row_ref[kh, ch] for ch in range(cps)] for kh in heads]

    def stack_rows(a):
        return jnp.concatenate([a[(i // rep) * c:(i // rep + 1) * c] for i in range(nb)], axis=0)

    k_big = [stack_rows(ka[kh]) for kh in heads]
    q_big = [stack_rows(qa[kh]) for kh in heads]
    v_big = [jnp.concatenate(
        [v_all[(i // rep) * c:(i // rep + 1) * c, (kh * rep + i % rep) * dh:(kh * rep + i % rep + 1) * dh]
         for i in range(nb)], axis=0) for kh in heads]
    gcol = [jnp.concatenate([cols[kh][i // rep][:, (i % rep):(i % rep) + 1] for i in range(nb)], axis=0)
            for kh in heads]
    bcol = [jnp.concatenate([cols[kh][i // rep][:, rep + (i % rep):rep + (i % rep) + 1] for i in range(nb)],
                            axis=0) for kh in heads]
    grow = [jnp.concatenate([rws[kh][i // rep][(i % rep):(i % rep) + 1, :] for i in range(nb)], axis=1)
            for kh in heads]
    glast = [jnp.concatenate(
        [jnp.broadcast_to(rws[kh][i // rep][(i % rep):(i % rep) + 1, c - 1:c], (c, 1)) for i in range(nb)],
        axis=0) for kh in heads]

    decay = [jnp.exp(jnp.where(incl, gcol[kh] - grow[kh], -jnp.inf)) for kh in heads]
    kk = [_dot_nt(k_big[kh], k_big[kh]) for kh in heads]
    a_mat = [jnp.where(strict, kk[kh] * decay[kh] * bcol[kh], 0.0) for kh in heads]
    t_inv = [eye - a_mat[kh] for kh in heads]
    pw = a_mat
    for _ in range(steps):
        pw = [_dot(pw[kh], pw[kh]) for kh in heads]
        t_inv = [t_inv[kh] + _dot(t_inv[kh], pw[kh]) for kh in heads]
    egc = [jnp.exp(gcol[kh]) for kh in heads]
    rhs = [jnp.concatenate([v_big[kh] * bcol[kh], k_big[kh] * (bcol[kh] * egc[kh])], axis=1) for kh in heads]
    sol = [_dot(t_inv[kh], rhs[kh]) for kh in heads]
    qk = [_dot_nt(q_big[kh], k_big[kh]) * decay[kh] for kh in heads]
    qe = [q_big[kh] * egc[kh] for kh in heads]
    kd = [k_big[kh] * jnp.exp(glast[kh] - gcol[kh]) for kh in heads]
    eg_last = [jnp.exp(glast[kh]) for kh in heads]

    chains = [(kh, h) for kh in heads for h in range(rep)]
    for ch in range(cps):
        def blk(kh, h):
            return slice((ch * rep + h) * c, (ch * rep + h + 1) * c)
        wq_s = [_dot(jnp.concatenate([sol[kh][blk(kh, h), dh:], qe[kh][blk(kh, h)]], axis=0),
                     s_all[kh * rep + h]) for kh, h in chains]
        v_new = [sol[kh][blk(kh, h), :dh] - wq_s[j][:c] for j, (kh, h) in enumerate(chains)]
        o = [wq_s[j][c:] + _dot(qk[kh][blk(kh, h), blk(kh, h)], v_new[j]) for j, (kh, h) in enumerate(chains)]
        upd = [_dot(kd[kh][blk(kh, h)].T, v_new[j]) for j, (kh, h) in enumerate(chains)]
        for j, (kh, h) in enumerate(chains):
            i0 = (ch * rep + h) * c
            s_all[kh * rep + h] = s_all[kh * rep + h] * eg_last[kh][i0:i0 + 1, :] + upd[j]
        o = [v * lax.rsqrt(jnp.mean(v * v, axis=-1, keepdims=True) + RMS_EPS) * ng for v in o]
        zc = _silu(z_ref[ch * c:(ch + 1) * c, :])
        o_ref[ch * c:(ch + 1) * c, :] = (jnp.concatenate(o, axis=1) * zc).astype(BF16)

    for i in range(hpb * rep):
        s_ref[i] = s_all[i]

    @pl.when(step == n_steps - 1)
    def _():
        sout_ref[0] = s_ref[...]


def _gdn_scan(proj, colf, rowf, conv_w, conv_buf, norm_g, s0, *, row0, bsz, seq, cps, hpb, key_dim, val_dim):
    dh, c = GDN_HEAD_DIM, GDN_CHUNK
    hk = key_dim // dh
    rep = val_dim // key_dim
    kw = hpb * dh
    vw = hpb * rep * dh
    rows = cps * c
    n_steps = seq // rows
    r0 = row0 // rows
    kb = key_dim // kw
    vb = 2 * key_dim // vw
    zb = (2 * key_dim + val_dim) // vw

    def rowblk(s, h, t):
        return r0 + s * n_steps + t

    in_specs = [
        pl.BlockSpec((hpb, cps, c, 2 * rep), lambda s, h, t: (h, rowblk(s, h, t), 0, 0)),
        pl.BlockSpec((hpb, cps, 2 * rep, c), lambda s, h, t: (h, rowblk(s, h, t), 0, 0)),
        pl.BlockSpec((rows, kw), lambda s, h, t: (rowblk(s, h, t), h)),
        pl.BlockSpec((rows, kw), lambda s, h, t: (rowblk(s, h, t), kb + h)),
        pl.BlockSpec((rows, vw), lambda s, h, t: (rowblk(s, h, t), vb + h)),
        pl.BlockSpec((rows, vw), lambda s, h, t: (rowblk(s, h, t), zb + h)),
        pl.BlockSpec((CONV_W, kw), lambda s, h, t: (0, h)),
        pl.BlockSpec((CONV_W, kw), lambda s, h, t: (0, kb + h)),
        pl.BlockSpec((CONV_W, vw), lambda s, h, t: (0, vb + h)),
        pl.BlockSpec((1, CONV_W - 1, kw), lambda s, h, t: (s, 0, h)),
        pl.BlockSpec((1, CONV_W - 1, kw), lambda s, h, t: (s, 0, kb + h)),
        pl.BlockSpec((1, CONV_W - 1, vw), lambda s, h, t: (s, 0, vb + h)),
        pl.BlockSpec((1, dh), lambda s, h, t: (0, 0)),
        pl.BlockSpec((1, hpb * rep, dh, dh), lambda s, h, t: (s, h, 0, 0)),
    ]
    out_specs = [
        pl.BlockSpec((rows, vw), lambda s, h, t: (s * n_steps + t, h)),
        pl.BlockSpec((1, hpb * rep, dh, dh), lambda s, h, t: (s, h, 0, 0)),
    ]
    out_shape = [jax.ShapeDtypeStruct((bsz * seq, val_dim), BF16),
                 jax.ShapeDtypeStruct(s0.shape, F32)]
    scratch = [pltpu.VMEM((hpb * rep, dh, dh), F32),
               pltpu.VMEM((rows + CONV_TAIL, kw), F32),
               pltpu.VMEM((rows + CONV_TAIL, kw), F32),
               pltpu.VMEM((rows + CONV_TAIL, vw), F32)]
    m = cps * rep * c
    vmem = rows * (2 * kw + 2 * vw) * 4 * 6 + hpb * m * m * 4 * 12
    return pl.pallas_call(
        functools.partial(_gdn_kernel, cps=cps, n_steps=n_steps, rep=rep, hpb=hpb),
        grid=(bsz, hk // hpb, n_steps),
        in_specs=in_specs, out_specs=out_specs, out_shape=out_shape, scratch_shapes=scratch,
        compiler_params=_params(("arbitrary", "arbitrary", "arbitrary"), vmem),
    )(colf, rowf, proj, proj, proj, proj, conv_w, conv_w, conv_w,
      conv_buf, conv_buf, conv_buf, norm_g.reshape(1, dh), s0)


def _split_bf16(a):
    hi = a.astype(BF16)
    lo = (a - hi.astype(F32)).astype(BF16)
    return hi, lo


def _router_kernel(x_ref, wt_ref, b_ref, e_ref, gate_ref, rank_ref, cnt_ref, carry_ref, *, n_steps):
    i = pl.program_id(0)
    n_e = wt_ref.shape[0]
    tr = x_ref.shape[0]

    @pl.when(i == 0)
    def _():
        carry_ref[...] = jnp.zeros_like(carry_ref)

    xh, xl = _split_bf16(x_ref[...])
    wh, wl = _split_bf16(wt_ref[...])
    nt = (((1,), (1,)), ((), ()))
    logits = (lax.dot_general(wh, xh, nt, preferred_element_type=F32)
              + lax.dot_general(wh, xl, nt, preferred_element_type=F32)
              + lax.dot_general(wl, xh, nt, preferred_element_type=F32)) + b_ref[...]

    eidx = lax.broadcasted_iota(jnp.int32, (n_e, tr), 0)
    work = logits
    tops, idxs = [], []
    for _ in range(TOP_K):
        mval = jnp.max(work, axis=0, keepdims=True)
        idx = jnp.min(jnp.where(work == mval, eidx, n_e), axis=0, keepdims=True)
        tops.append(mval)
        idxs.append(idx)
        work = jnp.where(eidx == idx, -jnp.inf, work)
    exps = [jnp.exp(t - tops[0]) for t in tops]
    den = exps[0]
    for e in exps[1:]:
        den = den + e

    onehot = jnp.zeros((n_e, tr), F32)
    for idx in idxs:
        onehot = onehot + (eidx == idx).astype(F32)
    rr = lax.broadcasted_iota(jnp.int32, (tr, tr), 0)
    cc = lax.broadcasted_iota(jnp.int32, (tr, tr), 1)
    upper = (rr <= cc).astype(BF16)
    incl = jnp.dot(onehot.astype(BF16), upper, preferred_element_type=F32)
    before = carry_ref[...][:, 0:1] + incl - onehot
    for k in range(TOP_K):
        e_ref[k:k + 1, :] = idxs[k]
        gate_ref[k:k + 1, :] = exps[k] / den
        rank = jnp.sum(jnp.where(eidx == idxs[k], before, 0.0), axis=0, keepdims=True)
        rank_ref[k:k + 1, :] = rank.astype(jnp.int32)
    carry_ref[...] = carry_ref[...] + incl[:, tr - 1:tr]

    @pl.when(i == n_steps - 1)
    def _():
        cnt_ref[...] = carry_ref[...].astype(jnp.int32)


def _router(x, w_router, b_router):
    t, d = x.shape
    n_e = w_router.shape[1]
    tr = _tile(t, 512, LANES)
    n_steps = t // tr
    tok = pl.BlockSpec((TOP_K, tr), lambda i: (0, i))
    return pl.pallas_call(
        functools.partial(_router_kernel, n_steps=n_steps),
        grid=(n_steps,),
        in_specs=[pl.BlockSpec((tr, d), lambda i: (i, 0)),
                  pl.BlockSpec((n_e, d), lambda i: (0, 0)),
                  pl.BlockSpec((n_e, 1), lambda i: (0, 0))],
        out_specs=[tok, tok, tok, pl.BlockSpec((n_e, LANES), lambda i: (0, 0))],
        out_shape=[jax.ShapeDtypeStruct((TOP_K, t), jnp.int32),
                   jax.ShapeDtypeStruct((TOP_K, t), F32),
                   jax.ShapeDtypeStruct((TOP_K, t), jnp.int32),
                   jax.ShapeDtypeStruct((n_e, LANES), jnp.int32)],
        scratch_shapes=[pltpu.VMEM((n_e, LANES), F32)],
        compiler_params=_params(("arbitrary",), tr * d * 4 * 4 + tr * tr * 8),
    )(x, w_router.T, b_router.reshape(n_e, 1))


def _dispatch_kernel(tok_ref, x_ref, xs_ref, buf_ref, sem, *, tg, n_steps):
    i = pl.program_id(0)
    slot = i % 2

    def issue(step, s):
        def body(r, carry):
            src = tok_ref[step * tg + r]
            pltpu.make_async_copy(x_ref.at[pl.ds(src, 1)], buf_ref.at[s, pl.ds(r, 1)], sem.at[s]).start()
            return carry
        lax.fori_loop(0, tg, body, 0)

    @pl.when(i == 0)
    def _():
        issue(0, 0)

    @pl.when(i + 1 < n_steps)
    def _():
        issue(i + 1, 1 - slot)

    def drain(r, carry):
        pltpu.make_async_copy(x_ref.at[pl.ds(0, 1)], buf_ref.at[slot, pl.ds(r, 1)], sem.at[slot]).wait()
        return carry
    lax.fori_loop(0, tg, drain, 0)

    xs_ref[...] = buf_ref[slot].astype(BF16)


def _dispatch(x, row_tok):
    t, d = x.shape
    n_rows = row_tok.shape[0]
    tg = _tile(n_rows, 256, 16)
    n_steps = n_rows // tg
    grid_spec = pltpu.PrefetchScalarGridSpec(
        num_scalar_prefetch=1,
        grid=(n_steps,),
        in_specs=[pl.BlockSpec(memory_space=pl.ANY)],
        out_specs=pl.BlockSpec((tg, d), lambda i, tok: (i, 0)),
        scratch_shapes=[pltpu.VMEM((2, tg, d), F32), pltpu.SemaphoreType.DMA((2,))],
    )
    return pl.pallas_call(
        functools.partial(_dispatch_kernel, tg=tg, n_steps=n_steps),
        grid_spec=grid_spec,
        out_shape=jax.ShapeDtypeStruct((n_rows, d), BF16),
        compiler_params=_params(("arbitrary",), 2 * tg * d * 4 + 3 * tg * d * 2),
    )(row_tok, x)


def _cast_rows(dst_ref, src_ref, k_rows, k_step):
    def body(c, carry):
        r = pl.multiple_of(c * k_step, k_step)
        dst_ref[pl.ds(r, k_step), :] = src_ref[0, pl.ds(r, k_step), :].astype(BF16)
        return carry
    lax.fori_loop(0, k_rows // k_step, body, 0)


def _is_new_expert(be_ref, blk):
    prev = be_ref[jnp.maximum(blk - 1, 0)]
    return (blk == 0) | (be_ref[blk] != prev)


def _row_steps(tm):
    quarter = tm // 4
    if tm % 4 == 0 and quarter % 16 == 0:
        return tuple(quarter * i for i in range(1, 5))
    return (tm,)


def _for_filled_rows(rv_ref, blk, o_ref, compute):
    tm = o_ref.shape[0]
    filled = rv_ref[blk]
    lo = 0
    for r in _row_steps(tm):
        @pl.when((filled > lo) & (filled <= r))
        def _(r=r):
            o_ref[0:r, :] = compute(r)
            if r < tm:
                o_ref[r:tm, :] = jnp.zeros((tm - r, o_ref.shape[1]), o_ref.dtype)
        lo = r

    @pl.when(filled == 0)
    def _():
        o_ref[...] = jnp.zeros_like(o_ref)


def _gate_up_kernel(be_ref, rv_ref, x_ref, wg_ref, wu_ref, bg_ref, bu_ref, o_ref, wgb_ref, wub_ref,
                    *, k_rows, k_step):
    blk = pl.program_id(1)

    @pl.when(_is_new_expert(be_ref, blk))
    def _():
        _cast_rows(wgb_ref, wg_ref, k_rows, k_step)
        _cast_rows(wub_ref, wu_ref, k_rows, k_step)

    def compute(r):
        xb = x_ref[0:r, :]
        hg = jnp.dot(xb, wgb_ref[...], preferred_element_type=F32) + bg_ref[0]
        hu = jnp.dot(xb, wub_ref[...], preferred_element_type=F32) + bu_ref[0]
        hg = jnp.minimum(hg, SWIGLU_LIMIT)
        hu = jnp.clip(hu, -SWIGLU_LIMIT, SWIGLU_LIMIT)
        return ((hu + 1.0) * hg * jax.nn.sigmoid(SWIGLU_ALPHA * hg)).astype(BF16)

    _for_filled_rows(rv_ref, blk, o_ref, compute)


def _moe_gate_up(xs, w_gate_up, b_gate_up, layer, block_e, rows_filled, tm, tn_pref=256):
    n_rows, d = xs.shape
    n_e = w_gate_up.shape[1]
    f = w_gate_up.shape[3] // 2
    tn = _tile(f, tn_pref, LANES)
    nj = f // tn
    k_step = _tile(d, 256, 16)
    w4 = w_gate_up
    b4 = b_gate_up.reshape(b_gate_up.shape[0] * n_e, 1, 2 * f)
    w3 = w4.reshape(w4.shape[0] * n_e, d, 2 * f)
    e0 = layer * n_e
    grid_spec = pltpu.PrefetchScalarGridSpec(
        num_scalar_prefetch=2,
        grid=(nj, n_rows // tm),
        in_specs=[
            pl.BlockSpec((tm, d), lambda j, b, be, nv: (b, 0)),
            pl.BlockSpec((1, d, tn), lambda j, b, be, nv: (e0 + be[b], 0, j)),
            pl.BlockSpec((1, d, tn), lambda j, b, be, nv: (e0 + be[b], 0, nj + j)),
            pl.BlockSpec((1, 1, tn), lambda j, b, be, nv: (e0 + be[b], 0, j)),
            pl.BlockSpec((1, 1, tn), lambda j, b, be, nv: (e0 + be[b], 0, nj + j)),
        ],
        out_specs=pl.BlockSpec((tm, tn), lambda j, b, be, nv: (b, j)),
        scratch_shapes=[pltpu.VMEM((d, tn), BF16), pltpu.VMEM((d, tn), BF16)],
    )
    vmem = tm * d * 2 * 2 + d * tn * 4 * 4 + d * tn * 2 * 2 + tm * tn * 2 * 2 + tm * tn * 4 * 4
    return pl.pallas_call(
        functools.partial(_gate_up_kernel, k_rows=d, k_step=k_step),
        grid_spec=grid_spec,
        out_shape=jax.ShapeDtypeStruct((n_rows, f), BF16),
        compiler_params=_params(("arbitrary", "arbitrary"), vmem),
    )(block_e, rows_filled, xs, w3, w3, b4, b4)


def _down_kernel(be_ref, rv_ref, a_ref, w_ref, b_ref, o_ref, wb_ref, *, k_rows, k_step):
    blk = pl.program_id(1)

    @pl.when(_is_new_expert(be_ref, blk))
    def _():
        _cast_rows(wb_ref, w_ref, k_rows, k_step)

    def compute(r):
        return jnp.dot(a_ref[0:r, :], wb_ref[...], preferred_element_type=F32) + b_ref[0]

    _for_filled_rows(rv_ref, blk, o_ref, compute)


def _moe_down(act, w_down, b_down, layer, block_e, rows_filled, tm, tn_pref=512):
    n_rows, f = act.shape
    n_e = w_down.shape[1]
    d = w_down.shape[3]
    tn = _tile(d, tn_pref, LANES)
    k_step = _tile(f, 256, 16)
    w3 = w_down.reshape(w_down.shape[0] * n_e, f, d)
    b3 = b_down.reshape(b_down.shape[0] * n_e, 1, d)
    e0 = layer * n_e
    grid_spec = pltpu.PrefetchScalarGridSpec(
        num_scalar_prefetch=2,
        grid=(d // tn, n_rows // tm),
        in_specs=[
            pl.BlockSpec((tm, f), lambda j, b, be, nv: (b, 0)),
            pl.BlockSpec((1, f, tn), lambda j, b, be, nv: (e0 + be[b], 0, j)),
            pl.BlockSpec((1, 1, tn), lambda j, b, be, nv: (e0 + be[b], 0, j)),
        ],
        out_specs=pl.BlockSpec((tm, tn), lambda j, b, be, nv: (b, j)),
        scratch_shapes=[pltpu.VMEM((f, tn), BF16)],
    )
    vmem = tm * f * 2 * 2 + f * tn * 4 * 2 + f * tn * 2 + tm * tn * 4 * 3
    return pl.pallas_call(
        functools.partial(_down_kernel, k_rows=f, k_step=k_step),
        grid_spec=grid_spec,
        out_shape=jax.ShapeDtypeStruct((n_rows, d), F32),
        compiler_params=_params(("arbitrary", "arbitrary"), vmem),
    )(block_e, rows_filled, act, w3, b3)


def _combine_kernel(pos_ref, y_ref, x_ref, gate_ref, g_ref, b_ref, o_ref, ob_ref, buf_ref, sem,
                    *, tt, n_steps, alpha):
    i = pl.program_id(0)
    slot = i % 2

    def row_copy(step, t, k, s):
        src = pos_ref[(step * tt + t) * TOP_K + k]
        return pltpu.make_async_copy(y_ref.at[pl.ds(src, 1)], buf_ref.at[s, k, pl.ds(t, 1)], sem.at[s])

    def issue(step, s):
        def body(t, carry):
            for k in range(TOP_K):
                row_copy(step, t, k, s).start()
            return carry
        lax.fori_loop(0, tt, body, 0)

    @pl.when(i == 0)
    def _():
        issue(0, 0)

    @pl.when(i + 1 < n_steps)
    def _():
        issue(i + 1, 1 - slot)

    def drain(t, carry):
        for k in range(TOP_K):
            pltpu.make_async_copy(y_ref.at[pl.ds(0, 1)], buf_ref.at[slot, k, pl.ds(t, 1)], sem.at[slot]).wait()
        return carry
    lax.fori_loop(0, tt, drain, 0)

    gates = gate_ref[...]
    acc = alpha * x_ref[...]
    for k in range(TOP_K):
        acc = acc + buf_ref[slot, k] * gates[:, k:k + 1]
    out = _ln_rows(acc, g_ref[...], b_ref[...])
    o_ref[...] = out
    ob_ref[...] = out.astype(BF16)


def _moe_combine_ln(y_rows, pos_flat, gates_t, x, g, b, alpha):
    t, d = x.shape
    tt = _tile(t, 64, 16)
    n_steps = t // tt
    row = pl.BlockSpec((tt, d), lambda i, pos: (i, 0))
    vec = pl.BlockSpec((1, d), lambda i, pos: (0, 0))
    grid_spec = pltpu.PrefetchScalarGridSpec(
        num_scalar_prefetch=1,
        grid=(n_steps,),
        in_specs=[pl.BlockSpec(memory_space=pl.ANY), row,
                  pl.BlockSpec((tt, TOP_K), lambda i, pos: (i, 0)), vec, vec],
        out_specs=[row, row],
        scratch_shapes=[pltpu.VMEM((2, TOP_K, tt, d), F32), pltpu.SemaphoreType.DMA((2,))],
    )
    vmem = 2 * TOP_K * tt * d * 4 + tt * d * 4 * 8
    return pl.pallas_call(
        functools.partial(_combine_kernel, tt=tt, n_steps=n_steps, alpha=alpha),
        grid_spec=grid_spec,
        out_shape=[jax.ShapeDtypeStruct((t, d), F32), jax.ShapeDtypeStruct((t, d), BF16)],
        compiler_params=_params(("arbitrary",), vmem),
    )(pos_flat, y_rows, x, gates_t, g.reshape(1, d), b.reshape(1, d))


def _moe_layer(x, layer, w_router, b_router, w_gate_up, b_gate_up, w_down, b_down, ln_g, ln_b, alpha,
               tm_pref=512):
    t, d = x.shape
    n_e = w_router.shape[2]
    tm = _tile(t, tm_pref, 16)
    top_e, gates, rank, cnt = _router(x, w_router[layer], b_router[layer])
    counts = cnt[:, 0]
    padded = (counts + tm - 1) // tm * tm
    pad_end = jnp.cumsum(padded)
    pad_start = pad_end - padded
    n_blocks = (t * TOP_K) // tm + n_e
    n_rows = n_blocks * tm
    start_of = jnp.sum(jnp.where(top_e[..., None] == jnp.arange(n_e, dtype=jnp.int32), pad_start, 0), axis=-1)
    pos = (start_of + rank).T.reshape(-1).astype(jnp.int32)
    row_tok = jnp.zeros((n_rows,), jnp.int32).at[pos].set(
        jnp.arange(t * TOP_K, dtype=jnp.int32) // TOP_K, unique_indices=True)
    blk_row0 = jnp.arange(n_blocks, dtype=jnp.int32) * tm
    block_e = jnp.searchsorted(pad_end, jnp.minimum(blk_row0, pad_end[-1] - 1), side='right')
    block_e = jnp.minimum(block_e, n_e - 1).astype(jnp.int32)
    rows_filled = jnp.clip(pad_start[block_e] + counts[block_e] - blk_row0, 0, tm).astype(jnp.int32)

    xs = _dispatch(x, row_tok)
    act = _moe_gate_up(xs, w_gate_up, b_gate_up, layer, block_e, rows_filled, tm)
    y_rows = _moe_down(act, w_down, b_down, layer, block_e, rows_filled, tm)
    return _moe_combine_ln(y_rows, pos, gates.T, x, ln_g[layer, 1], ln_b[layer, 1], alpha)


def _last_rows(proj, row0, bsz, seq, col0, col1):
    ends = [row0 + (s + 1) * seq for s in range(bsz)]
    return jnp.stack([proj[e - (CONV_W - 1):e, col0:col1] for e in ends])


def _per_head_forms(col_arrays, n_groups, q):
    t = col_arrays[0].shape[0]
    parts = [a.reshape(t // q, q, n_groups, -1) for a in col_arrays]
    colf = jnp.concatenate(parts, axis=-1).transpose(2, 0, 1, 3)
    return colf, colf.transpose(0, 1, 3, 2)


def _ssd_layer(xb, groups, state, conv_cache, w_in, conv_w, conv_b, dt_bias, a_log, d_skip, norm_g, w_out):
    inner = w_out.shape[0]
    heads = inner // SSD_HEAD_DIM
    gn = SSD_GROUPS * SSD_STATE
    main_cols = 2 * inner + 2 * gn
    proj = _matmul(xb, w_in, 0, main_cols, tm_pref=768)
    dt_raw = _matmul(xb, w_in[:, main_cols:], 0, heads)
    d_exp = jnp.repeat(d_skip, SSD_HEAD_DIM).reshape(1, inner)
    ys, hs, caches = [], [], []
    for (row0, bsz, seq, q), h0, cbuf in zip(groups, state, conv_cache):
        dt, cum = _ssd_prep(dt_raw, dt_bias, a_log, q, row0, bsz * seq)
        colf, rowf = _per_head_forms([cum, dt], SSD_GROUPS, q)
        r = heads // SSD_GROUPS
        y, h = _ssd_scan(proj, colf[..., :r], colf[..., r:], rowf[:, :, :r], conv_w, conv_b, cbuf,
                         d_exp, norm_g, h0, row0=row0, bsz=bsz, seq=seq, q=q, inner=inner)
        ys.append(y)
        hs.append(h)
        caches.append(_last_rows(proj, row0, bsz, seq, inner, 2 * inner + 2 * gn))
    mix = _matmul(jnp.concatenate(ys, axis=0), w_out, 0, w_out.shape[1], tn_pref=256)
    return mix, hs, caches


def _gdn_layer(xb, groups, state, conv_cache, w_in, conv_w, a_log, dt_bias, norm_g, w_out):
    val_dim = w_out.shape[0]
    hv = a_log.shape[0]
    key_dim = (conv_w.shape[1] - val_dim) // 2
    hk = key_dim // GDN_HEAD_DIM
    main_cols = 2 * key_dim + 2 * val_dim
    proj = _matmul(xb, w_in, 0, main_cols, tm_pref=768)
    ba = _matmul(xb, w_in[:, main_cols:], 0, 2 * hv)
    beta, gc = _gdn_prep(ba, a_log, dt_bias, GDN_CHUNK)
    colf, rowf = _per_head_forms([gc, beta], hk, GDN_CHUNK)
    os_, ss, caches = [], [], []
    for (row0, bsz, seq, cps), s0, cbuf in zip(groups, state, conv_cache):
        o, s = _gdn_scan(proj, colf, rowf, conv_w, cbuf, norm_g, s0, row0=row0, bsz=bsz, seq=seq,
                         cps=cps, hpb=_tile(hk, GDN_HEADS_PER_PROGRAM, 1), key_dim=key_dim, val_dim=val_dim)
        os_.append(o)
        ss.append(s)
        caches.append(_last_rows(proj, row0, bsz, seq, 0, 2 * key_dim + val_dim))
    mix = _matmul(jnp.concatenate(os_, axis=0), w_out, 0, w_out.shape[1], tn_pref=256)
    return mix, ss, caches


def kernel(x_prompt, x_sample, state_ssd, cache_ssd_conv, state_gdn, cache_gdn_conv, ssd_w_in, ssd_conv_w, ssd_conv_b, ssd_dt_bias, ssd_a_log, ssd_d, ssd_norm_g, ssd_w_out, gdn_w_in, gdn_conv_w, gdn_a_log, gdn_dt_bias, gdn_norm_g, gdn_w_out, moe_w_router, moe_b_router, moe_w_gate_up, moe_b_gate_up, moe_w_down, moe_b_down, ln_g, ln_b):
    bp, lp, d = x_prompt.shape
    bs, ls, _ = x_sample.shape
    depth = ln_g.shape[0]
    alpha = (2 * depth) ** 0.25
    tp = bp * lp
    x = jnp.concatenate([x_prompt.reshape(tp, d), x_sample.reshape(bs * ls, d)], axis=0)
    xb = x.astype(BF16)

    ssd_q_prompt = _tile(lp, 256, GDN_CHUNK)
    ssd_q_sample = _tile(ls, 256, GDN_CHUNK)
    if tp % ssd_q_sample or tp % ssd_q_prompt:
        raise ValueError("prompt rows must be chunk aligned")
    gdn_cps_prompt = _tile(lp // GDN_CHUNK, 2, 1)
    gdn_cps_sample = _tile(ls // GDN_CHUNK, 2, 1)
    ssd_groups = [(0, bp, lp, ssd_q_prompt), (tp, bs, ls, ssd_q_sample)]
    gdn_groups = [(0, bp, lp, gdn_cps_prompt), (tp, bs, ls, gdn_cps_sample)]

    ssd_h, ssd_c, gdn_h, gdn_c = [], [], [], []
    for i in range(depth):
        j = i // 2
        if i % 2 == 0:
            zero_h = jnp.zeros((bp,) + state_ssd.shape[2:], F32)
            zero_c = jnp.zeros((bp,) + cache_ssd_conv.shape[2:], F32)
            mix, hs, cs = _ssd_layer(xb, ssd_groups, [zero_h, state_ssd[j]], [zero_c, cache_ssd_conv[j]],
                                     ssd_w_in[j], ssd_conv_w[j], ssd_conv_b[j], ssd_dt_bias[j], ssd_a_log[j],
                                     ssd_d[j], ssd_norm_g[j], ssd_w_out[j])
            ssd_h.append(hs)
            ssd_c.append(cs)
        else:
            zero_h = jnp.zeros((bp,) + state_gdn.shape[2:], F32)
            zero_c = jnp.zeros((bp,) + cache_gdn_conv.shape[2:], F32)
            mix, hs, cs = _gdn_layer(xb, gdn_groups, [zero_h, state_gdn[j]], [zero_c, cache_gdn_conv[j]],
                                     gdn_w_in[j], gdn_conv_w[j], gdn_a_log[j], gdn_dt_bias[j], gdn_norm_g[j],
                                     gdn_w_out[j])
            gdn_h.append(hs)
            gdn_c.append(cs)
        x, xb = _deepnorm_ln(x, mix, ln_g[i, 0], ln_b[i, 0], alpha)
        x, xb = _moe_layer(x, i, moe_w_router, moe_b_router, moe_w_gate_up, moe_b_gate_up, moe_w_down,
                           moe_b_down, ln_g, ln_b, alpha)

    def stack(items, which):
        return jnp.stack([it[which] for it in items])

    y_prompt = x[:tp].reshape(bp, lp, d)
    y_sample = x[tp:].reshape(bs, ls, d)
    return (y_prompt, y_sample,
            stack(ssd_h, 0), stack(ssd_c, 0), stack(gdn_h, 0), stack(gdn_c, 0),
            stack(ssd_h, 1), stack(ssd_c, 1), stack(gdn_h, 1), stack(gdn_c, 1))
```

```python
import functools
import math

import jax
import jax.numpy as jnp
from jax import lax
from jax.experimental import pallas as pl
from jax.experimental.pallas import tpu as pltpu

F32 = jnp.float32
BF16 = jnp.bfloat16

CONV_W = 4
LN_EPS = 1e-5
RMS_EPS = 1e-5
L2_EPS = 1e-6
SSD_HEAD_DIM = 64
SSD_GROUPS = 8
SSD_STATE = 128
GDN_HEAD_DIM = 128
GDN_CHUNK = 64
GDN_HEADS_PER_PROGRAM = 4
TOP_K = 4
SWIGLU_ALPHA = 1.702
SWIGLU_LIMIT = 7.0
MOE_ROW_PIECE = 512

LANES = 128
SUBLANES = 8
VMEM_CAP_BYTES = 60000 * 1024
CONV_TAIL = SUBLANES


def _tile(dim, pref, align):
    t = min(pref, dim) // align * align
    while t >= align:
        if dim % t == 0:
            return t
        t -= align
    return dim


def _params(sem, vmem_bytes):
    limit = int(min(VMEM_CAP_BYTES, max(32 * 1024 * 1024, vmem_bytes * 5 // 4 + (4 << 20))))
    return pltpu.CompilerParams(dimension_semantics=sem, vmem_limit_bytes=limit)


def _silu(v):
    return v * jax.nn.sigmoid(v)


def _softplus(v):
    return jnp.maximum(v, 0.0) + jnp.log1p(jnp.exp(-jnp.abs(v)))


def _dot(a, b):
    return jnp.dot(a.astype(BF16), b.astype(BF16), preferred_element_type=F32)


def _dot_nt(a, b):
    return lax.dot_general(a.astype(BF16), b.astype(BF16), (((1,), (1,)), ((), ())),
                           preferred_element_type=F32)


def _dot_f32(a, b):
    return jnp.dot(a, b, preferred_element_type=F32, precision=lax.Precision.HIGHEST)


def _mm_kernel(x_ref, w_ref, o_ref, wb_ref, *, k_rows, k_step):
    @pl.when(pl.program_id(1) == 0)
    def _():
        def body(c, carry):
            r = pl.multiple_of(c * k_step, k_step)
            wb_ref[pl.ds(r, k_step), :] = w_ref[pl.ds(r, k_step), :].astype(BF16)
            return carry
        lax.fori_loop(0, k_rows // k_step, body, 0)

    o_ref[...] = jnp.dot(x_ref[...], wb_ref[...], preferred_element_type=F32)


def _matmul(x, w, col0, ncols, tm_pref=512, tn_pref=512, w_buffers=2):
    m, k = x.shape
    tm = _tile(m, tm_pref, 16)
    tn = _tile(math.gcd(ncols, col0) if col0 else ncols, tn_pref, LANES)
    if tn % LANES and not (col0 == 0 and ncols == w.shape[1]):
        raise ValueError("projection column range must be lane aligned")
    k_step = _tile(k, 256, 16)
    j0 = col0 // tn
    vmem = tm * k * 2 * 2 + k * tn * 4 * w_buffers + k * tn * 2 + tm * tn * 4 * 2 + tm * tn * 4
    w_mode = {} if w_buffers == 2 else {"pipeline_mode": pl.Buffered(w_buffers)}
    return pl.pallas_call(
        functools.partial(_mm_kernel, k_rows=k, k_step=k_step),
        grid=(ncols // tn, m // tm),
        in_specs=[pl.BlockSpec((tm, k), lambda j, i: (i, 0)),
                  pl.BlockSpec((k, tn), lambda j, i: (0, j + j0), **w_mode)],
        out_specs=pl.BlockSpec((tm, tn), lambda j, i: (i, j)),
        out_shape=jax.ShapeDtypeStruct((m, ncols), F32),
        scratch_shapes=[pltpu.VMEM((k, tn), BF16)],
        compiler_params=_params(("arbitrary", "arbitrary"), vmem),
    )(x, w)


def _ln_rows(v, g, b):
    mu = jnp.mean(v, axis=-1, keepdims=True)
    vc = v - mu
    var = jnp.mean(vc * vc, axis=-1, keepdims=True)
    return vc * lax.rsqrt(var + LN_EPS) * g + b


def _ln_kernel(x_ref, y_ref, g_ref, b_ref, o_ref, ob_ref, *, alpha):
    out = _ln_rows(alpha * x_ref[...] + y_ref[...], g_ref[...], b_ref[...])
    o_ref[...] = out
    ob_ref[...] = out.astype(BF16)


def _deepnorm_ln(x, y, g, b, alpha):
    t, d = x.shape
    tr = _tile(t, 256, 16)
    row = pl.BlockSpec((tr, d), lambda i: (i, 0))
    vec = pl.BlockSpec((1, d), lambda i: (0, 0))
    return pl.pallas_call(
        functools.partial(_ln_kernel, alpha=alpha),
        grid=(t // tr,),
        in_specs=[row, row, vec, vec],
        out_specs=[row, row],
        out_shape=[jax.ShapeDtypeStruct((t, d), F32), jax.ShapeDtypeStruct((t, d), BF16)],
        compiler_params=_params(("parallel",), tr * d * 4 * 8),
    )(x, y, g.reshape(1, d), b.reshape(1, d))


def _tri_incl(q):
    r = lax.broadcasted_iota(jnp.int32, (q, q), 0)
    c = lax.broadcasted_iota(jnp.int32, (q, q), 1)
    return (r >= c).astype(F32)


def _ssd_prep_kernel(dt_ref, bias_ref, alog_ref, dto_ref, cum_ref, *, q):
    dt = _softplus(dt_ref[...] + bias_ref[...])
    dto_ref[...] = dt
    cum_ref[...] = _dot_f32(_tri_incl(q), dt * (-jnp.exp(alog_ref[...])))


def _ssd_prep(dt_raw, dt_bias, a_log, q, row0, n_rows):
    h = dt_raw.shape[1]
    c0 = row0 // q
    row = pl.BlockSpec((q, h), lambda i: (i, 0))
    vec = pl.BlockSpec((1, h), lambda i: (0, 0))
    return pl.pallas_call(
        functools.partial(_ssd_prep_kernel, q=q),
        grid=(n_rows // q,),
        in_specs=[pl.BlockSpec((q, h), lambda i: (c0 + i, 0)), vec, vec],
        out_specs=[row, row],
        out_shape=[jax.ShapeDtypeStruct((n_rows, h), F32)] * 2,
        compiler_params=_params(("parallel",), 1 << 20),
    )(dt_raw, dt_bias.reshape(1, h), a_log.reshape(1, h))


def _gdn_prep_kernel(ba_ref, alog_ref, bias_ref, beta_ref, gc_ref, *, q, hv):
    ba = ba_ref[...]
    beta_ref[...] = jax.nn.sigmoid(ba[:, :hv])
    g = -jnp.exp(alog_ref[...]) * _softplus(ba[:, hv:] + bias_ref[...])
    gc_ref[...] = _dot_f32(_tri_incl(q), g)


def _gdn_prep(ba, a_log, dt_bias, q):
    t, hv2 = ba.shape
    hv = hv2 // 2
    vec = pl.BlockSpec((1, hv), lambda i: (0, 0))
    out = pl.BlockSpec((q, hv), lambda i: (i, 0))
    return pl.pallas_call(
        functools.partial(_gdn_prep_kernel, q=q, hv=hv),
        grid=(t // q,),
        in_specs=[pl.BlockSpec((q, hv2), lambda i: (i, 0)), vec, vec],
        out_specs=[out, out],
        out_shape=[jax.ShapeDtypeStruct((t, hv), F32)] * 2,
        compiler_params=_params(("parallel",), 1 << 20),
    )(ba, a_log.reshape(1, hv), dt_bias.reshape(1, hv))


def _conv_init(cat_ref, buf_ref):
    cat_ref[0:CONV_TAIL, :] = jnp.zeros((CONV_TAIL, cat_ref.shape[1]), F32)
    cat_ref[CONV_TAIL - (CONV_W - 1):CONV_TAIL, :] = buf_ref[0]


def _conv_window(cat_ref, u, w_ref, rows):
    cat_ref[CONV_TAIL:CONV_TAIL + rows, :] = u
    base = CONV_TAIL - (CONV_W - 1)
    out = cat_ref[base:base + rows, :] * w_ref[0:1, :]
    for j in range(1, CONV_W):
        out = out + cat_ref[base + j:base + j + rows, :] * w_ref[j:j + 1, :]
    cat_ref[0:CONV_TAIL, :] = cat_ref[rows:rows + CONV_TAIL, :]
    return out


def _ssd_kernel(cumc_ref, dtc_ref, cumr_ref, z_ref, x_ref, b_ref, c_ref,
                wx_ref, wb_ref, wc_ref, bx_ref, bb_ref, bc_ref, ux_ref, ub_ref, uc_ref,
                d_ref, ng_ref, h0_ref,
                y_ref, hout_ref,
                hs_ref, catx_ref, catb_ref, catc_ref, ybuf_ref, *, q, n_pairs, n_chunks):
    c_idx = pl.program_id(2)
    p2 = 2 * SSD_HEAD_DIM

    @pl.when(c_idx == 0)
    def _():
        hs_ref[...] = h0_ref[0]
        _conv_init(catx_ref, ux_ref)
        _conv_init(catb_ref, ub_ref)
        _conv_init(catc_ref, uc_ref)

    xs = _silu(_conv_window(catx_ref, x_ref[...], wx_ref, q) + bx_ref[...])
    bm = _silu(_conv_window(catb_ref, b_ref[...], wb_ref, q) + bb_ref[...])
    cm = _silu(_conv_window(catc_ref, c_ref[...], wc_ref, q) + bc_ref[...])

    cumc = cumc_ref[0, 0]
    dtc = dtc_ref[0, 0]
    cumr = cumr_ref[0, 0]
    cb = _dot_nt(cm, bm)
    rr = lax.broadcasted_iota(jnp.int32, (q, q), 0)
    cc = lax.broadcasted_iota(jnp.int32, (q, q), 1)
    tril = rr >= cc
    lo = lax.broadcasted_iota(jnp.int32, (q, p2), 1) < SSD_HEAD_DIM
    lo_row = lax.broadcasted_iota(jnp.int32, (1, p2), 1) < SSD_HEAD_DIM
    top = lax.broadcasted_iota(jnp.int32, (p2, 1), 0) < SSD_HEAD_DIM
    cm_b = cm.astype(BF16)
    bm_b = bm.astype(BF16)

    for pair in range(n_pairs):
        ra, rb = 2 * pair, 2 * pair + 1
        ca, cbk = cumc[:, ra:ra + 1], cumc[:, rb:rb + 1]
        cum_pair = jnp.where(lo, ca, cbk)
        dt_pair = jnp.where(lo, dtc[:, ra:ra + 1], dtc[:, rb:rb + 1])
        last_a = cumr[ra:ra + 1, q - 1:q]
        last_b = cumr[rb:rb + 1, q - 1:q]
        last_pair = jnp.where(lo_row, last_a, last_b)
        xdt = xs[:, pair * p2:(pair + 1) * p2] * dt_pair
        xw = xdt * jnp.exp(last_pair - cum_pair)
        m_a = cb * jnp.exp(jnp.where(tril, ca - cumr[ra:ra + 1, :], -jnp.inf))
        m_b = cb * jnp.exp(jnp.where(tril, cbk - cumr[rb:rb + 1, :], -jnp.inf))
        y = _dot(m_a, jnp.where(lo, xdt, 0.0)) + _dot(m_b, jnp.where(lo, 0.0, xdt))
        hp = hs_ref[ra:ra + 2].reshape(p2, SSD_STATE)
        y = y + _dot_nt(cm_b, hp) * jnp.exp(cum_pair)
        ybuf_ref[:, pair * p2:(pair + 1) * p2] = y
        scale = jnp.where(top, jnp.exp(last_a), jnp.exp(last_b))
        hn = hp * scale + _dot(xw.T, bm_b)
        hs_ref[ra:ra + 2] = hn.reshape(2, SSD_HEAD_DIM, SSD_STATE)

    yf = (ybuf_ref[...] + xs * d_ref[...]) * _silu(z_ref[...])
    ms = jnp.mean(yf * yf, axis=-1, keepdims=True)
    y_ref[...] = (yf * lax.rsqrt(ms + RMS_EPS) * ng_ref[...]).astype(BF16)

    @pl.when(c_idx == n_chunks - 1)
    def _():
        hout_ref[0] = hs_ref[...]


def _ssd_scan(proj, cumc, dtc, cumr, conv_w, conv_b, conv_buf, d_exp, norm_g, h0,
              *, row0, bsz, seq, q, inner):
    g_n, n = SSD_GROUPS, SSD_STATE
    gw = inner // g_n
    r = gw // SSD_HEAD_DIM
    nc = seq // q
    c0 = row0 // q
    xb = inner // gw
    bb = 2 * inner // n
    cvb = inner // n

    def rowblk(s, g, c):
        return c0 + s * nc + c

    in_specs = [
        pl.BlockSpec((1, 1, q, r), lambda s, g, c: (g, s * nc + c, 0, 0)),
        pl.BlockSpec((1, 1, q, r), lambda s, g, c: (g, s * nc + c, 0, 0)),
        pl.BlockSpec((1, 1, r, q), lambda s, g, c: (g, s * nc + c, 0, 0)),
        pl.BlockSpec((q, gw), lambda s, g, c: (rowblk(s, g, c), g)),
        pl.BlockSpec((q, gw), lambda s, g, c: (rowblk(s, g, c), xb + g)),
        pl.BlockSpec((q, n), lambda s, g, c: (rowblk(s, g, c), bb + g)),
        pl.BlockSpec((q, n), lambda s, g, c: (rowblk(s, g, c), bb + g_n + g)),
        pl.BlockSpec((CONV_W, gw), lambda s, g, c: (0, g)),
        pl.BlockSpec((CONV_W, n), lambda s, g, c: (0, cvb + g)),
        pl.BlockSpec((CONV_W, n), lambda s, g, c: (0, cvb + g_n + g)),
        pl.BlockSpec((1, gw), lambda s, g, c: (0, g)),
        pl.BlockSpec((1, n), lambda s, g, c: (0, cvb + g)),
        pl.BlockSpec((1, n), lambda s, g, c: (0, cvb + g_n + g)),
        pl.BlockSpec((1, CONV_W - 1, gw), lambda s, g, c: (s, 0, g)),
        pl.BlockSpec((1, CONV_W - 1, n), lambda s, g, c: (s, 0, cvb + g)),
        pl.BlockSpec((1, CONV_W - 1, n), lambda s, g, c: (s, 0, cvb + g_n + g)),
        pl.BlockSpec((1, gw), lambda s, g, c: (0, g)),
        pl.BlockSpec((1, gw), lambda s, g, c: (0, g)),
        pl.BlockSpec((1, r, SSD_HEAD_DIM, n), lambda s, g, c: (s, g, 0, 0)),
    ]
    out_specs = [
        pl.BlockSpec((q, gw), lambda s, g, c: (s * nc + c, g)),
        pl.BlockSpec((1, r, SSD_HEAD_DIM, n), lambda s, g, c: (s, g, 0, 0)),
    ]
    out_shape = [jax.ShapeDtypeStruct((bsz * seq, inner), BF16),
                 jax.ShapeDtypeStruct(h0.shape, F32)]
    scratch = [pltpu.VMEM((r, SSD_HEAD_DIM, n), F32),
               pltpu.VMEM((q + CONV_TAIL, gw), F32),
               pltpu.VMEM((q + CONV_TAIL, n), F32),
               pltpu.VMEM((q + CONV_TAIL, n), F32),
               pltpu.VMEM((q, gw), F32)]
    vmem = q * gw * 4 * 12 + q * q * 4 * 8
    cb2 = conv_b.reshape(1, -1)
    return pl.pallas_call(
        functools.partial(_ssd_kernel, q=q, n_pairs=r // 2, n_chunks=nc),
        grid=(bsz, g_n, nc),
        in_specs=in_specs, out_specs=out_specs, out_shape=out_shape, scratch_shapes=scratch,
        compiler_params=_params(("arbitrary", "arbitrary", "arbitrary"), vmem),
    )(cumc, dtc, cumr, proj, proj, proj, proj, conv_w, conv_w, conv_w, cb2, cb2, cb2,
      conv_buf, conv_buf, conv_buf, d_exp, norm_g.reshape(1, -1), h0)


def _gdn_kernel(col_ref, row_ref, q_ref, k_ref, v_ref, z_ref, wq_ref, wk_ref, wv_ref,
                uq_ref, uk_ref, uv_ref, ng_ref, s0_ref,
                o_ref, sout_ref,
                s_ref, catq_ref, catk_ref, catv_ref, *, cps, n_steps, rep, hpb):
    c = GDN_CHUNK
    dh = GDN_HEAD_DIM
    vw = rep * dh
    rows = cps * c
    step = pl.program_id(2)

    @pl.when(step == 0)
    def _():
        s_ref[...] = s0_ref[0]
        _conv_init(catq_ref, uq_ref)
        _conv_init(catk_ref, uk_ref)
        _conv_init(catv_ref, uv_ref)

    q_all = _silu(_conv_window(catq_ref, q_ref[...], wq_ref, rows))
    k_all = _silu(_conv_window(catk_ref, k_ref[...], wk_ref, rows))
    v_all = _silu(_conv_window(catv_ref, v_ref[...], wv_ref, rows))

    nb = cps * rep
    m = nb * c
    ri = lax.broadcasted_iota(jnp.int32, (m, m), 0)
    ci = lax.broadcasted_iota(jnp.int32, (m, m), 1)
    shift = c.bit_length() - 1
    same = jnp.right_shift(ri, shift) == jnp.right_shift(ci, shift)
    incl = same & (ri >= ci)
    strict = same & (ri > ci)
    eye = (ri == ci).astype(F32)
    steps = max(1, int(math.ceil(math.log2(c))) - 1)

    s_all = [s_ref[i] for i in range(hpb * rep)]
    heads = range(hpb)
    ng = ng_ref[...]

    def l2n(a):
        return a * lax.rsqrt(jnp.sum(a * a, axis=-1, keepdims=True) + L2_EPS)

    qa = [l2n(q_all[:, kh * dh:(kh + 1) * dh]) * (dh ** -0.5) for kh in heads]
    ka = [l2n(k_all[:, kh * dh:(kh + 1) * dh]) for kh in heads]
    cols = [[col_ref[kh, ch] for ch in range(cps)] for kh in heads]
    rws = [[row_ref[kh, ch] for ch in range(cps)] for kh in heads]

    def stack_rows(a):
        return jnp.concatenate([a[(i // rep) * c:(i // rep + 1) * c] for i in range(nb)], axis=0)

    k_big = [stack_rows(ka[kh]) for kh in heads]
    q_big = [stack_rows(qa[kh]) for kh in heads]
    v_big = [jnp.concatenate(
        [v_all[(i // rep) * c:(i // rep + 1) * c, (kh * rep + i % rep) * dh:(kh * rep + i % rep + 1) * dh]
         for i in range(nb)], axis=0) for kh in heads]
    gcol = [jnp.concatenate([cols[kh][i // rep][:, (i % rep):(i % rep) + 1] for i in range(nb)], axis=0)
            for kh in heads]
    bcol = [jnp.concatenate([cols[kh][i // rep][:, rep + (i % rep):rep + (i % rep) + 1] for i in range(nb)],
                            axis=0) for kh in heads]
    grow = [jnp.concatenate([rws[kh][i // rep][(i % rep):(i % rep) + 1, :] for i in range(nb)], axis=1)
            for kh in heads]
    glast = [jnp.concatenate(
        [jnp.broadcast_to(rws[kh][i // rep][(i % rep):(i % rep) + 1, c - 1:c], (c, 1)) for i in range(nb)],
        axis=0) for kh in heads]

    decay = [jnp.exp(jnp.where(incl, gcol[kh] - grow[kh], -jnp.inf)) for kh in heads]
    kk = [_dot_nt(k_big[kh], k_big[kh]) for kh in heads]
    a_mat = [jnp.where(strict, kk[kh] * decay[kh] * bcol[kh], 0.0) for kh in heads]
    t_inv = [eye - a_mat[kh] for kh in heads]
    pw = a_mat
    for _ in range(steps):
        pw = [_dot(pw[kh], pw[kh]) for kh in heads]
        t_inv = [t_inv[kh] + _dot(t_inv[kh], pw[kh]) for kh in heads]
    egc = [jnp.exp(gcol[kh]) for kh in heads]
    rhs = [jnp.concatenate([v_big[kh] * bcol[kh], k_big[kh] * (bcol[kh] * egc[kh])], axis=1) for kh in heads]
    sol = [_dot(t_inv[kh], rhs[kh]) for kh in heads]
    qk = [_dot_nt(q_big[kh], k_big[kh]) * decay[kh] for kh in heads]
    qe = [q_big[kh] * egc[kh] for kh in heads]
    kd = [k_big[kh] * jnp.exp(glast[kh] - gcol[kh]) for kh in heads]
    eg_last = [jnp.exp(glast[kh]) for kh in heads]

    chains = [(kh, h) for kh in heads for h in range(rep)]
    for ch in range(cps):
        def blk(kh, h):
            return slice((ch * rep + h) * c, (ch * rep + h + 1) * c)
        wq_s = [_dot(jnp.concatenate([sol[kh][blk(kh, h), dh:], qe[kh][blk(kh, h)]], axis=0),
                     s_all[kh * rep + h]) for kh, h in chains]
        v_new = [sol[kh][blk(kh, h), :dh] - wq_s[j][:c] for j, (kh, h) in enumerate(chains)]
        o = [wq_s[j][c:] + _dot(qk[kh][blk(kh, h), blk(kh, h)], v_new[j]) for j, (kh, h) in enumerate(chains)]
        upd = [_dot(kd[kh][blk(kh, h)].T, v_new[j]) for j, (kh, h) in enumerate(chains)]
        for j, (kh, h) in enumerate(chains):
            i0 = (ch * rep + h) * c
            s_all[kh * rep + h] = s_all[kh * rep + h] * eg_last[kh][i0:i0 + 1, :] + upd[j]
        o = [v * lax.rsqrt(jnp.mean(v * v, axis=-1, keepdims=True) + RMS_EPS) * ng for v in o]
        zc = _silu(z_ref[ch * c:(ch + 1) * c, :])
        o_ref[ch * c:(ch + 1) * c, :] = (jnp.concatenate(o, axis=1) * zc).astype(BF16)

    for i in range(hpb * rep):
        s_ref[i] = s_all[i]

    @pl.when(step == n_steps - 1)
    def _():
        sout_ref[0] = s_ref[...]


def _gdn_scan(proj, colf, rowf, conv_w, conv_buf, norm_g, s0, *, row0, bsz, seq, cps, hpb, key_dim, val_dim):
    dh, c = GDN_HEAD_DIM, GDN_CHUNK
    hk = key_dim // dh
    rep = val_dim // key_dim
    kw = hpb * dh
    vw = hpb * rep * dh
    rows = cps * c
    n_steps = seq // rows
    r0 = row0 // rows
    kb = key_dim // kw
    vb = 2 * key_dim // vw
    zb = (2 * key_dim + val_dim) // vw

    def rowblk(s, h, t):
        return r0 + s * n_steps + t

    in_specs = [
        pl.BlockSpec((hpb, cps, c, 2 * rep), lambda s, h, t: (h, rowblk(s, h, t), 0, 0)),
        pl.BlockSpec((hpb, cps, 2 * rep, c), lambda s, h, t: (h, rowblk(s, h, t), 0, 0)),
        pl.BlockSpec((rows, kw), lambda s, h, t: (rowblk(s, h, t), h)),
        pl.BlockSpec((rows, kw), lambda s, h, t: (rowblk(s, h, t), kb + h)),
        pl.BlockSpec((rows, vw), lambda s, h, t: (rowblk(s, h, t), vb + h)),
        pl.BlockSpec((rows, vw), lambda s, h, t: (rowblk(s, h, t), zb + h)),
        pl.BlockSpec((CONV_W, kw), lambda s, h, t: (0, h)),
        pl.BlockSpec((CONV_W, kw), lambda s, h, t: (0, kb + h)),
        pl.BlockSpec((CONV_W, vw), lambda s, h, t: (0, vb + h)),
        pl.BlockSpec((1, CONV_W - 1, kw), lambda s, h, t: (s, 0, h)),
        pl.BlockSpec((1, CONV_W - 1, kw), lambda s, h, t: (s, 0, kb + h)),
        pl.BlockSpec((1, CONV_W - 1, vw), lambda s, h, t: (s, 0, vb + h)),
        pl.BlockSpec((1, dh), lambda s, h, t: (0, 0)),
        pl.BlockSpec((1, hpb * rep, dh, dh), lambda s, h, t: (s, h, 0, 0)),
    ]
    out_specs = [
        pl.BlockSpec((rows, vw), lambda s, h, t: (s * n_steps + t, h)),
        pl.BlockSpec((1, hpb * rep, dh, dh), lambda s, h, t: (s, h, 0, 0)),
    ]
    out_shape = [jax.ShapeDtypeStruct((bsz * seq, val_dim), BF16),
                 jax.ShapeDtypeStruct(s0.shape, F32)]
    scratch = [pltpu.VMEM((hpb * rep, dh, dh), F32),
               pltpu.VMEM((rows + CONV_TAIL, kw), F32),
               pltpu.VMEM((rows + CONV_TAIL, kw), F32),
               pltpu.VMEM((rows + CONV_TAIL, vw), F32)]
    m = cps * rep * c
    vmem = rows * (2 * kw + 2 * vw) * 4 * 6 + hpb * m * m * 4 * 12
    return pl.pallas_call(
        functools.partial(_gdn_kernel, cps=cps, n_steps=n_steps, rep=rep, hpb=hpb),
        grid=(bsz, hk // hpb, n_steps),
        in_specs=in_specs, out_specs=out_specs, out_shape=out_shape, scratch_shapes=scratch,
        compiler_params=_params(("arbitrary", "arbitrary", "arbitrary"), vmem),
    )(colf, rowf, proj, proj, proj, proj, conv_w, conv_w, conv_w,
      conv_buf, conv_buf, conv_buf, norm_g.reshape(1, dh), s0)


def _split_bf16(a):
    hi = a.astype(BF16)
    lo = (a - hi.astype(F32)).astype(BF16)
    return hi, lo


def _router_kernel(x_ref, wt_ref, b_ref, e_ref, gate_ref, rank_ref, cnt_ref, carry_ref, *, n_steps):
    i = pl.program_id(0)
    n_e = wt_ref.shape[0]
    tr = x_ref.shape[0]

    @pl.when(i == 0)
    def _():
        carry_ref[...] = jnp.zeros_like(carry_ref)

    xh, xl = _split_bf16(x_ref[...])
    wh, wl = _split_bf16(wt_ref[...])
    nt = (((1,), (1,)), ((), ()))
    logits = (lax.dot_general(wh, xh, nt, preferred_element_type=F32)
              + lax.dot_general(wh, xl, nt, preferred_element_type=F32)
              + lax.dot_general(wl, xh, nt, preferred_element_type=F32)) + b_ref[...]

    eidx = lax.broadcasted_iota(jnp.int32, (n_e, tr), 0)
    work = logits
    tops, idxs = [], []
    for _ in range(TOP_K):
        mval = jnp.max(work, axis=0, keepdims=True)
        idx = jnp.min(jnp.where(work == mval, eidx, n_e), axis=0, keepdims=True)
        tops.append(mval)
        idxs.append(idx)
        work = jnp.where(eidx == idx, -jnp.inf, work)
    exps = [jnp.exp(t - tops[0]) for t in tops]
    den = exps[0]
    for e in exps[1:]:
        den = den + e

    onehot = jnp.zeros((n_e, tr), F32)
    for idx in idxs:
        onehot = onehot + (eidx == idx).astype(F32)
    rr = lax.broadcasted_iota(jnp.int32, (tr, tr), 0)
    cc = lax.broadcasted_iota(jnp.int32, (tr, tr), 1)
    upper = (rr <= cc).astype(BF16)
    incl = jnp.dot(onehot.astype(BF16), upper, preferred_element_type=F32)
    before = carry_ref[...][:, 0:1] + incl - onehot
    for k in range(TOP_K):
        e_ref[k:k + 1, :] = idxs[k]
        gate_ref[k:k + 1, :] = exps[k] / den
        rank = jnp.sum(jnp.where(eidx == idxs[k], before, 0.0), axis=0, keepdims=True)
        rank_ref[k:k + 1, :] = rank.astype(jnp.int32)
    carry_ref[...] = carry_ref[...] + incl[:, tr - 1:tr]

    @pl.when(i == n_steps - 1)
    def _():
        cnt_ref[...] = carry_ref[...].astype(jnp.int32)


def _router(x, w_router, b_router):
    t, d = x.shape
    n_e = w_router.shape[1]
    tr = _tile(t, 512, LANES)
    n_steps = t // tr
    tok = pl.BlockSpec((TOP_K, tr), lambda i: (0, i))
    return pl.pallas_call(
        functools.partial(_router_kernel, n_steps=n_steps),
        grid=(n_steps,),
        in_specs=[pl.BlockSpec((tr, d), lambda i: (i, 0)),
                  pl.BlockSpec((n_e, d), lambda i: (0, 0)),
                  pl.BlockSpec((n_e, 1), lambda i: (0, 0))],
        out_specs=[tok, tok, tok, pl.BlockSpec((n_e, LANES), lambda i: (0, 0))],
        out_shape=[jax.ShapeDtypeStruct((TOP_K, t), jnp.int32),
                   jax.ShapeDtypeStruct((TOP_K, t), F32),
                   jax.ShapeDtypeStruct((TOP_K, t), jnp.int32),
                   jax.ShapeDtypeStruct((n_e, LANES), jnp.int32)],
        scratch_shapes=[pltpu.VMEM((n_e, LANES), F32)],
        compiler_params=_params(("arbitrary",), tr * d * 4 * 4 + tr * tr * 8),
    )(x, w_router.T, b_router.reshape(n_e, 1))


def _dispatch_kernel(tok_ref, used_ref, x_ref, xs_ref, buf_ref, sem, *, tg, n_steps):
    i = pl.program_id(0)
    slot = i % 2
    n_used = used_ref[0]

    def issue(step, s):
        def body(r, carry):
            src = tok_ref[step * tg + r]
            pltpu.make_async_copy(x_ref.at[pl.ds(src, 1)], buf_ref.at[s, pl.ds(r, 1)], sem.at[s]).start()
            return carry
        lax.fori_loop(0, tg, body, 0, unroll=8)

    @pl.when((i == 0) & (n_used > 0))
    def _():
        issue(0, 0)

    @pl.when((i + 1 < n_steps) & ((i + 1) * tg < n_used))
    def _():
        issue(i + 1, 1 - slot)

    @pl.when(i * tg < n_used)
    def _():
        def drain(r, carry):
            pltpu.make_async_copy(x_ref.at[pl.ds(0, 1)], buf_ref.at[slot, pl.ds(r, 1)], sem.at[slot]).wait()
            return carry
        lax.fori_loop(0, tg, drain, 0, unroll=8)
        xs_ref[...] = buf_ref[slot].astype(BF16)

    @pl.when(i * tg >= n_used)
    def _():
        xs_ref[...] = jnp.zeros_like(xs_ref)


def _dispatch(x, row_tok, n_used):
    t, d = x.shape
    n_rows = row_tok.shape[0]
    tg = _tile(n_rows, 256, 16)
    n_steps = n_rows // tg
    grid_spec = pltpu.PrefetchScalarGridSpec(
        num_scalar_prefetch=2,
        grid=(n_steps,),
        in_specs=[pl.BlockSpec(memory_space=pl.ANY)],
        out_specs=pl.BlockSpec((tg, d), lambda i, tok, used: (i, 0)),
        scratch_shapes=[pltpu.VMEM((2, tg, d), F32), pltpu.SemaphoreType.DMA((2,))],
    )
    return pl.pallas_call(
        functools.partial(_dispatch_kernel, tg=tg, n_steps=n_steps),
        grid_spec=grid_spec,
        out_shape=jax.ShapeDtypeStruct((n_rows, d), BF16),
        compiler_params=_params(("arbitrary",), 2 * tg * d * 4 + 3 * tg * d * 2),
    )(row_tok, n_used, x)


def _cast_rows(dst_ref, src_ref, k_rows, k_step):
    def body(c, carry):
        r = pl.multiple_of(c * k_step, k_step)
        dst_ref[pl.ds(r, k_step), :] = src_ref[0, pl.ds(r, k_step), :].astype(BF16)
        return carry
    lax.fori_loop(0, k_rows // k_step, body, 0)


def _is_new_expert(be_ref, blk):
    prev = be_ref[jnp.maximum(blk - 1, 0)]
    return (blk == 0) | (be_ref[blk] != prev)


def _row_steps(tm):
    quarter = tm // 4
    if tm % 4 == 0 and quarter % 16 == 0:
        return tuple(quarter * i for i in range(1, 5))
    return (tm,)


def _for_filled_rows(rv_ref, blk, o_ref, compute):
    tm = o_ref.shape[0]
    filled = rv_ref[blk]
    lo = 0
    for r in _row_steps(tm):
        @pl.when((filled > lo) & (filled <= r))
        def _(r=r):
            o_ref[0:r, :] = compute(r)
            if r < tm:
                o_ref[r:tm, :] = jnp.zeros((tm - r, o_ref.shape[1]), o_ref.dtype)
        lo = r

    @pl.when(filled == 0)
    def _():
        o_ref[...] = jnp.zeros_like(o_ref)


def _gate_up_kernel(be_ref, rv_ref, x_ref, wg_ref, wu_ref, bg_ref, bu_ref, o_ref, wgb_ref, wub_ref,
                    *, k_rows, k_step):
    blk = pl.program_id(1)

    @pl.when(_is_new_expert(be_ref, blk))
    def _():
        _cast_rows(wgb_ref, wg_ref, k_rows, k_step)
        _cast_rows(wub_ref, wu_ref, k_rows, k_step)

    def compute(r):
        cuts = list(range(0, r, MOE_ROW_PIECE)) + [r]
        pieces = [x_ref[a:b, :] for a, b in zip(cuts[:-1], cuts[1:])]
        hs = [(jnp.dot(xb, wgb_ref[...], preferred_element_type=F32),
               jnp.dot(xb, wub_ref[...], preferred_element_type=F32)) for xb in pieces]
        outs = []
        for hg, hu in hs:
            hg = jnp.minimum(hg + bg_ref[0], SWIGLU_LIMIT)
            hu = jnp.clip(hu + bu_ref[0], -SWIGLU_LIMIT, SWIGLU_LIMIT)
            outs.append(((hu + 1.0) * hg * jax.nn.sigmoid(SWIGLU_ALPHA * hg)).astype(BF16))
        return outs[0] if len(outs) == 1 else jnp.concatenate(outs, axis=0)

    _for_filled_rows(rv_ref, blk, o_ref, compute)


def _moe_gate_up(xs, w_gate_up, b_gate_up, layer, block_e, rows_filled, tm, tn_pref=256):
    n_rows, d = xs.shape
    n_e = w_gate_up.shape[1]
    f = w_gate_up.shape[3] // 2
    tn = _tile(f, tn_pref, LANES)
    nj = f // tn
    k_step = _tile(d, 256, 16)
    w4 = w_gate_up
    b4 = b_gate_up.reshape(b_gate_up.shape[0] * n_e, 1, 2 * f)
    w3 = w4.reshape(w4.shape[0] * n_e, d, 2 * f)
    e0 = layer * n_e
    grid_spec = pltpu.PrefetchScalarGridSpec(
        num_scalar_prefetch=2,
        grid=(nj, n_rows // tm),
        in_specs=[
            pl.BlockSpec((tm, d), lambda j, b, be, nv: (b, 0)),
            pl.BlockSpec((1, d, tn), lambda j, b, be, nv: (e0 + be[b], 0, j)),
            pl.BlockSpec((1, d, tn), lambda j, b, be, nv: (e0 + be[b], 0, nj + j)),
            pl.BlockSpec((1, 1, tn), lambda j, b, be, nv: (e0 + be[b], 0, j)),
            pl.BlockSpec((1, 1, tn), lambda j, b, be, nv: (e0 + be[b], 0, nj + j)),
        ],
        out_specs=pl.BlockSpec((tm, tn), lambda j, b, be, nv: (b, j)),
        scratch_shapes=[pltpu.VMEM((d, tn), BF16), pltpu.VMEM((d, tn), BF16)],
    )
    vmem = tm * d * 2 * 2 + d * tn * 4 * 4 + d * tn * 2 * 2 + tm * tn * 2 * 2 + tm * tn * 4 * 4
    return pl.pallas_call(
        functools.partial(_gate_up_kernel, k_rows=d, k_step=k_step),
        grid_spec=grid_spec,
        out_shape=jax.ShapeDtypeStruct((n_rows, f), BF16),
        compiler_params=_params(("arbitrary", "arbitrary"), vmem),
    )(block_e, rows_filled, xs, w3, w3, b4, b4)


def _down_kernel(be_ref, rv_ref, a_ref, w_ref, b_ref, o_ref, wb_ref, *, k_rows, k_step):
    blk = pl.program_id(1)

    @pl.when(_is_new_expert(be_ref, blk))
    def _():
        _cast_rows(wb_ref, w_ref, k_rows, k_step)

    def compute(r):
        return jnp.dot(a_ref[0:r, :], wb_ref[...], preferred_element_type=F32) + b_ref[0]

    _for_filled_rows(rv_ref, blk, o_ref, compute)


def _moe_down(act, w_down, b_down, layer, block_e, rows_filled, tm, tn_pref=512):
    n_rows, f = act.shape
    n_e = w_down.shape[1]
    d = w_down.shape[3]
    tn = _tile(d, tn_pref, LANES)
    k_step = _tile(f, 256, 16)
    w3 = w_down.reshape(w_down.shape[0] * n_e, f, d)
    b3 = b_down.reshape(b_down.shape[0] * n_e, 1, d)
    e0 = layer * n_e
    grid_spec = pltpu.PrefetchScalarGridSpec(
        num_scalar_prefetch=2,
        grid=(d // tn, n_rows // tm),
        in_specs=[
            pl.BlockSpec((tm, f), lambda j, b, be, nv: (b, 0)),
            pl.BlockSpec((1, f, tn), lambda j, b, be, nv: (e0 + be[b], 0, j)),
            pl.BlockSpec((1, 1, tn), lambda j, b, be, nv: (e0 + be[b], 0, j)),
        ],
        out_specs=pl.BlockSpec((tm, tn), lambda j, b, be, nv: (b, j)),
        scratch_shapes=[pltpu.VMEM((f, tn), BF16)],
    )
    vmem = tm * f * 2 * 2 + f * tn * 4 * 2 + f * tn * 2 + tm * tn * 4 * 3
    return pl.pallas_call(
        functools.partial(_down_kernel, k_rows=f, k_step=k_step),
        grid_spec=grid_spec,
        out_shape=jax.ShapeDtypeStruct((n_rows, d), F32),
        compiler_params=_params(("arbitrary", "arbitrary"), vmem),
    )(block_e, rows_filled, act, w3, b3)


def _combine_kernel(pos_ref, y_ref, x_ref, gate_ref, g_ref, b_ref, o_ref, ob_ref, buf_ref, sem,
                    *, tt, n_steps, alpha):
    i = pl.program_id(0)
    slot = i % 2

    def row_copy(step, t, k, s):
        src = pos_ref[(step * tt + t) * TOP_K + k]
        return pltpu.make_async_copy(y_ref.at[pl.ds(src, 1)], buf_ref.at[s, k, pl.ds(t, 1)], sem.at[s])

    def issue(step, s):
        def body(t, carry):
            for k in range(TOP_K):
                row_copy(step, t, k, s).start()
            return carry
        lax.fori_loop(0, tt, body, 0)

    @pl.when(i == 0)
    def _():
        issue(0, 0)

    @pl.when(i + 1 < n_steps)
    def _():
        issue(i + 1, 1 - slot)

    def drain(t, carry):
        for k in range(TOP_K):
            pltpu.make_async_copy(y_ref.at[pl.ds(0, 1)], buf_ref.at[slot, k, pl.ds(t, 1)], sem.at[slot]).wait()
        return carry
    lax.fori_loop(0, tt, drain, 0)

    gates = gate_ref[...]
    acc = alpha * x_ref[...]
    for k in range(TOP_K):
        acc = acc + buf_ref[slot, k] * gates[:, k:k + 1]
    out = _ln_rows(acc, g_ref[...], b_ref[...])
    o_ref[...] = out
    ob_ref[...] = out.astype(BF16)


def _moe_combine_ln(y_rows, pos_flat, gates_t, x, g, b, alpha):
    t, d = x.shape
    tt = _tile(t, 64, 16)
    n_steps = t // tt
    row = pl.BlockSpec((tt, d), lambda i, pos: (i, 0))
    vec = pl.BlockSpec((1, d), lambda i, pos: (0, 0))
    grid_spec = pltpu.PrefetchScalarGridSpec(
        num_scalar_prefetch=1,
        grid=(n_steps,),
        in_specs=[pl.BlockSpec(memory_space=pl.ANY), row,
                  pl.BlockSpec((tt, TOP_K), lambda i, pos: (i, 0)), vec, vec],
        out_specs=[row, row],
        scratch_shapes=[pltpu.VMEM((2, TOP_K, tt, d), F32), pltpu.SemaphoreType.DMA((2,))],
    )
    vmem = 2 * TOP_K * tt * d * 4 + tt * d * 4 * 8
    return pl.pallas_call(
        functools.partial(_combine_kernel, tt=tt, n_steps=n_steps, alpha=alpha),
        grid_spec=grid_spec,
        out_shape=[jax.ShapeDtypeStruct((t, d), F32), jax.ShapeDtypeStruct((t, d), BF16)],
        compiler_params=_params(("arbitrary",), vmem),
    )(pos_flat, y_rows, x, gates_t, g.reshape(1, d), b.reshape(1, d))


def _moe_layer(x, layer, w_router, b_router, w_gate_up, b_gate_up, w_down, b_down, ln_g, ln_b, alpha,
               tm_pref=1024):
    t, d = x.shape
    n_e = w_router.shape[2]
    tm = _tile(t, tm_pref, 16)
    top_e, gates, rank, cnt = _router(x, w_router[layer], b_router[layer])
    counts = cnt[:, 0]
    padded = (counts + tm - 1) // tm * tm
    pad_end = jnp.cumsum(padded)
    pad_start = pad_end - padded
    n_blocks = (t * TOP_K) // tm + n_e
    n_rows = n_blocks * tm
    start_of = jnp.sum(jnp.where(top_e[..., None] == jnp.arange(n_e, dtype=jnp.int32), pad_start, 0), axis=-1)
    pos = (start_of + rank).T.reshape(-1).astype(jnp.int32)
    row_tok = jnp.zeros((n_rows,), jnp.int32).at[pos].set(
        jnp.arange(t * TOP_K, dtype=jnp.int32) // TOP_K, unique_indices=True)
    blk_row0 = jnp.arange(n_blocks, dtype=jnp.int32) * tm
    last_row = jnp.minimum(blk_row0, pad_end[-1] - 1)
    block_e = jnp.sum(pad_end[None, :] <= last_row[:, None], axis=1)
    block_e = jnp.minimum(block_e, n_e - 1).astype(jnp.int32)
    rows_filled = jnp.clip(pad_start[block_e] + counts[block_e] - blk_row0, 0, tm).astype(jnp.int32)

    xs = _dispatch(x, row_tok, pad_end[-1:].astype(jnp.int32))
    act = _moe_gate_up(xs, w_gate_up, b_gate_up, layer, block_e, rows_filled, tm)
    y_rows = _moe_down(act, w_down, b_down, layer, block_e, rows_filled, tm)
    return _moe_combine_ln(y_rows, pos, gates.T, x, ln_g[layer, 1], ln_b[layer, 1], alpha)


def _last_rows(proj, row0, bsz, seq, col0, col1):
    ends = [row0 + (s + 1) * seq for s in range(bsz)]
    return jnp.stack([proj[e - (CONV_W - 1):e, col0:col1] for e in ends])


def _per_head_forms(col_arrays, n_groups, q):
    t = col_arrays[0].shape[0]
    parts = [a.reshape(t // q, q, n_groups, -1) for a in col_arrays]
    colf = jnp.concatenate(parts, axis=-1).transpose(2, 0, 1, 3)
    return colf, colf.transpose(0, 1, 3, 2)


def _ssd_layer(xb, groups, state, conv_cache, w_in, conv_w, conv_b, dt_bias, a_log, d_skip, norm_g, w_out):
    inner = w_out.shape[0]
    heads = inner // SSD_HEAD_DIM
    gn = SSD_GROUPS * SSD_STATE
    main_cols = 2 * inner + 2 * gn
    proj = _matmul(xb, w_in, 0, main_cols, tm_pref=768)
    dt_raw = _matmul(xb, w_in[:, main_cols:], 0, heads)
    d_exp = jnp.repeat(d_skip, SSD_HEAD_DIM).reshape(1, inner)
    ys, hs, caches = [], [], []
    for (row0, bsz, seq, q), h0, cbuf in zip(groups, state, conv_cache):
        dt, cum = _ssd_prep(dt_raw, dt_bias, a_log, q, row0, bsz * seq)
        colf, rowf = _per_head_forms([cum, dt], SSD_GROUPS, q)
        r = heads // SSD_GROUPS
        y, h = _ssd_scan(proj, colf[..., :r], colf[..., r:], rowf[:, :, :r], conv_w, conv_b, cbuf,
                         d_exp, norm_g, h0, row0=row0, bsz=bsz, seq=seq, q=q, inner=inner)
        ys.append(y)
        hs.append(h)
        caches.append(_last_rows(proj, row0, bsz, seq, inner, 2 * inner + 2 * gn))
    mix = _matmul(jnp.concatenate(ys, axis=0), w_out, 0, w_out.shape[1], w_buffers=1)
    return mix, hs, caches


def _gdn_layer(xb, groups, state, conv_cache, w_in, conv_w, a_log, dt_bias, norm_g, w_out):
    val_dim = w_out.shape[0]
    hv = a_log.shape[0]
    key_dim = (conv_w.shape[1] - val_dim) // 2
    hk = key_dim // GDN_HEAD_DIM
    main_cols = 2 * key_dim + 2 * val_dim
    proj = _matmul(xb, w_in, 0, main_cols, tm_pref=768)
    ba = _matmul(xb, w_in[:, main_cols:], 0, 2 * hv)
    beta, gc = _gdn_prep(ba, a_log, dt_bias, GDN_CHUNK)
    colf, rowf = _per_head_forms([gc, beta], hk, GDN_CHUNK)
    os_, ss, caches = [], [], []
    for (row0, bsz, seq, cps), s0, cbuf in zip(groups, state, conv_cache):
        o, s = _gdn_scan(proj, colf, rowf, conv_w, cbuf, norm_g, s0, row0=row0, bsz=bsz, seq=seq,
                         cps=cps, hpb=_tile(hk, GDN_HEADS_PER_PROGRAM, 1), key_dim=key_dim, val_dim=val_dim)
        os_.append(o)
        ss.append(s)
        caches.append(_last_rows(proj, row0, bsz, seq, 0, 2 * key_dim + val_dim))
    mix = _matmul(jnp.concatenate(os_, axis=0), w_out, 0, w_out.shape[1], w_buffers=1)
    return mix, ss, caches


def kernel(x_prompt, x_sample, state_ssd, cache_ssd_conv, state_gdn, cache_gdn_conv, ssd_w_in, ssd_conv_w, ssd_conv_b, ssd_dt_bias, ssd_a_log, ssd_d, ssd_norm_g, ssd_w_out, gdn_w_in, gdn_conv_w, gdn_a_log, gdn_dt_bias, gdn_norm_g, gdn_w_out, moe_w_router, moe_b_router, moe_w_gate_up, moe_b_gate_up, moe_w_down, moe_b_down, ln_g, ln_b):
    bp, lp, d = x_prompt.shape
    bs, ls, _ = x_sample.shape
    depth = ln_g.shape[0]
    alpha = (2 * depth) ** 0.25
    tp = bp * lp
    x = jnp.concatenate([x_prompt.reshape(tp, d), x_sample.reshape(bs * ls, d)], axis=0)
    xb = x.astype(BF16)

    ssd_q_prompt = _tile(lp, 256, GDN_CHUNK)
    ssd_q_sample = _tile(ls, 256, GDN_CHUNK)
    if tp % ssd_q_sample or tp % ssd_q_prompt:
        raise ValueError("prompt rows must be chunk aligned")
    gdn_cps_prompt = _tile(lp // GDN_CHUNK, 2, 1)
    gdn_cps_sample = _tile(ls // GDN_CHUNK, 2, 1)
    ssd_groups = [(0, bp, lp, ssd_q_prompt), (tp, bs, ls, ssd_q_sample)]
    gdn_groups = [(0, bp, lp, gdn_cps_prompt), (tp, bs, ls, gdn_cps_sample)]

    ssd_h, ssd_c, gdn_h, gdn_c = [], [], [], []
    for i in range(depth):
        j = i // 2
        if i % 2 == 0:
            zero_h = jnp.zeros((bp,) + state_ssd.shape[2:], F32)
            zero_c = jnp.zeros((bp,) + cache_ssd_conv.shape[2:], F32)
            mix, hs, cs = _ssd_layer(xb, ssd_groups, [zero_h, state_ssd[j]], [zero_c, cache_ssd_conv[j]],
                                     ssd_w_in[j], ssd_conv_w[j], ssd_conv_b[j], ssd_dt_bias[j], ssd_a_log[j],
                                     ssd_d[j], ssd_norm_g[j], ssd_w_out[j])
            ssd_h.append(hs)
            ssd_c.append(cs)
        else:
            zero_h = jnp.zeros((bp,) + state_gdn.shape[2:], F32)
            zero_c = jnp.zeros((bp,) + cache_gdn_conv.shape[2:], F32)
            mix, hs, cs = _gdn_layer(xb, gdn_groups, [zero_h, state_gdn[j]], [zero_c, cache_gdn_conv[j]],
                                     gdn_w_in[j], gdn_conv_w[j], gdn_a_log[j], gdn_dt_bias[j], gdn_norm_g[j],
                                     gdn_w_out[j])
            gdn_h.append(hs)
            gdn_c.append(cs)
        x, xb = _deepnorm_ln(x, mix, ln_g[i, 0], ln_b[i, 0], alpha)
        x, xb = _moe_layer(x, i, moe_w_router, moe_b_router, moe_w_gate_up, moe_b_gate_up, moe_w_down,
                           moe_b_down, ln_g, ln_b, alpha)

    def stack(items, which):
        return jnp.stack([it[which] for it in items])

    y_prompt = x[:tp].reshape(bp, lp, d)
    y_sample = x[tp:].reshape(bs, ls, d)
    return (y_prompt, y_sample,
            stack(ssd_h, 0), stack(ssd_c, 0), stack(gdn_h, 0), stack(gdn_c, 0),
            stack(ssd_h, 1), stack(ssd_c, 1), stack(gdn_h, 1), stack(gdn_c, 1))
```

```python
import functools
import math

import jax
import jax.numpy as jnp
from jax import lax
from jax.experimental import pallas as pl
from jax.experimental.pallas import tpu as pltpu

F32 = jnp.float32
BF16 = jnp.bfloat16

CONV_W = 4
LN_EPS = 1e-5
RMS_EPS = 1e-5
L2_EPS = 1e-6
SSD_HEAD_DIM = 64
SSD_GROUPS = 8
SSD_STATE = 128
GDN_HEAD_DIM = 128
GDN_CHUNK = 64
GDN_HEADS_PER_PROGRAM = 4
TOP_K = 4
SWIGLU_ALPHA = 1.702
SWIGLU_LIMIT = 7.0
MOE_ROW_PIECE = 512

LANES = 128
SUBLANES = 8
VMEM_CAP_BYTES = 60000 * 1024
CONV_TAIL = SUBLANES


def _tile(dim, pref, align):
    t = min(pref, dim) // align * align
    while t >= align:
        if dim % t == 0:
            return t
        t -= align
    return dim


def _params(sem, vmem_bytes):
    limit = int(min(VMEM_CAP_BYTES, max(32 * 1024 * 1024, vmem_bytes * 5 // 4 + (4 << 20))))
    return pltpu.CompilerParams(dimension_semantics=sem, vmem_limit_bytes=limit)


def _silu(v):
    return v * jax.nn.sigmoid(v)


def _softplus(v):
    return jnp.maximum(v, 0.0) + jnp.log1p(jnp.exp(-jnp.abs(v)))


def _dot(a, b):
    return jnp.dot(a.astype(BF16), b.astype(BF16), preferred_element_type=F32)


def _dot_nt(a, b):
    return lax.dot_general(a.astype(BF16), b.astype(BF16), (((1,), (1,)), ((), ())),
                           preferred_element_type=F32)


def _dot_f32(a, b):
    return jnp.dot(a, b, preferred_element_type=F32, precision=lax.Precision.HIGHEST)


def _mm_kernel(x_ref, w_ref, o_ref, wb_ref, *, k_rows, k_step):
    @pl.when(pl.program_id(1) == 0)
    def _():
        def body(c, carry):
            r = pl.multiple_of(c * k_step, k_step)
            wb_ref[pl.ds(r, k_step), :] = w_ref[pl.ds(r, k_step), :].astype(BF16)
            return carry
        lax.fori_loop(0, k_rows // k_step, body, 0)

    o_ref[...] = jnp.dot(x_ref[...], wb_ref[...], preferred_element_type=F32)


def _matmul(x, w, col0, ncols, tm_pref=512, tn_pref=512, w_buffers=2):
    m, k = x.shape
    tm = _tile(m, tm_pref, 16)
    tn = _tile(math.gcd(ncols, col0) if col0 else ncols, tn_pref, LANES)
    if tn % LANES and not (col0 == 0 and ncols == w.shape[1]):
        raise ValueError("projection column range must be lane aligned")
    k_step = _tile(k, 256, 16)
    j0 = col0 // tn
    vmem = tm * k * 2 * 2 + k * tn * 4 * w_buffers + k * tn * 2 + tm * tn * 4 * 2 + tm * tn * 4
    w_mode = {} if w_buffers == 2 else {"pipeline_mode": pl.Buffered(w_buffers)}
    return pl.pallas_call(
        functools.partial(_mm_kernel, k_rows=k, k_step=k_step),
        grid=(ncols // tn, m // tm),
        in_specs=[pl.BlockSpec((tm, k), lambda j, i: (i, 0)),
                  pl.BlockSpec((k, tn), lambda j, i: (0, j + j0), **w_mode)],
        out_specs=pl.BlockSpec((tm, tn), lambda j, i: (i, j)),
        out_shape=jax.ShapeDtypeStruct((m, ncols), F32),
        scratch_shapes=[pltpu.VMEM((k, tn), BF16)],
        compiler_params=_params(("arbitrary", "arbitrary"), vmem),
    )(x, w)


def _ln_rows(v, g, b):
    mu = jnp.mean(v, axis=-1, keepdims=True)
    vc = v - mu
    var = jnp.mean(vc * vc, axis=-1, keepdims=True)
    return vc * lax.rsqrt(var + LN_EPS) * g + b


def _ln_kernel(x_ref, y_ref, g_ref, b_ref, o_ref, ob_ref, *, alpha):
    out = _ln_rows(alpha * x_ref[...] + y_ref[...], g_ref[...], b_ref[...])
    o_ref[...] = out
    ob_ref[...] = out.astype(BF16)


def _deepnorm_ln(x, y, g, b, alpha):
    t, d = x.shape
    tr = _tile(t, 256, 16)
    row = pl.BlockSpec((tr, d), lambda i: (i, 0))
    vec = pl.BlockSpec((1, d), lambda i: (0, 0))
    return pl.pallas_call(
        functools.partial(_ln_kernel, alpha=alpha),
        grid=(t // tr,),
        in_specs=[row, row, vec, vec],
        out_specs=[row, row],
        out_shape=[jax.ShapeDtypeStruct((t, d), F32), jax.ShapeDtypeStruct((t, d), BF16)],
        compiler_params=_params(("parallel",), tr * d * 4 * 8),
    )(x, y, g.reshape(1, d), b.reshape(1, d))


def _tri_incl(q):
    r = lax.broadcasted_iota(jnp.int32, (q, q), 0)
    c = lax.broadcasted_iota(jnp.int32, (q, q), 1)
    return (r >= c).astype(F32)


def _ssd_prep_kernel(dt_ref, bias_ref, alog_ref, dto_ref, cum_ref, *, q):
    dt = _softplus(dt_ref[...] + bias_ref[...])
    dto_ref[...] = dt
    cum_ref[...] = _dot_f32(_tri_incl(q), dt * (-jnp.exp(alog_ref[...])))


def _ssd_prep(dt_raw, dt_bias, a_log, q, row0, n_rows):
    h = dt_raw.shape[1]
    c0 = row0 // q
    row = pl.BlockSpec((q, h), lambda i: (i, 0))
    vec = pl.BlockSpec((1, h), lambda i: (0, 0))
    return pl.pallas_call(
        functools.partial(_ssd_prep_kernel, q=q),
        grid=(n_rows // q,),
        in_specs=[pl.BlockSpec((q, h), lambda i: (c0 + i, 0)), vec, vec],
        out_specs=[row, row],
        out_shape=[jax.ShapeDtypeStruct((n_rows, h), F32)] * 2,
        compiler_params=_params(("parallel",), 1 << 20),
    )(dt_raw, dt_bias.reshape(1, h), a_log.reshape(1, h))


def _gdn_prep_kernel(ba_ref, alog_ref, bias_ref, beta_ref, gc_ref, *, q, hv):
    ba = ba_ref[...]
    beta_ref[...] = jax.nn.sigmoid(ba[:, :hv])
    g = -jnp.exp(alog_ref[...]) * _softplus(ba[:, hv:] + bias_ref[...])
    gc_ref[...] = _dot_f32(_tri_incl(q), g)


def _gdn_prep(ba, a_log, dt_bias, q):
    t, hv2 = ba.shape
    hv = hv2 // 2
    vec = pl.BlockSpec((1, hv), lambda i: (0, 0))
    out = pl.BlockSpec((q, hv), lambda i: (i, 0))
    return pl.pallas_call(
        functools.partial(_gdn_prep_kernel, q=q, hv=hv),
        grid=(t // q,),
        in_specs=[pl.BlockSpec((q, hv2), lambda i: (i, 0)), vec, vec],
        out_specs=[out, out],
        out_shape=[jax.ShapeDtypeStruct((t, hv), F32)] * 2,
        compiler_params=_params(("parallel",), 1 << 20),
    )(ba, a_log.reshape(1, hv), dt_bias.reshape(1, hv))


def _conv_init(cat_ref, buf_ref):
    cat_ref[0:CONV_TAIL, :] = jnp.zeros((CONV_TAIL, cat_ref.shape[1]), F32)
    cat_ref[CONV_TAIL - (CONV_W - 1):CONV_TAIL, :] = buf_ref[0]


def _conv_window(cat_ref, u, w_ref, rows):
    cat_ref[CONV_TAIL:CONV_TAIL + rows, :] = u
    base = CONV_TAIL - (CONV_W - 1)
    out = cat_ref[base:base + rows, :] * w_ref[0:1, :]
    for j in range(1, CONV_W):
        out = out + cat_ref[base + j:base + j + rows, :] * w_ref[j:j + 1, :]
    cat_ref[0:CONV_TAIL, :] = cat_ref[rows:rows + CONV_TAIL, :]
    return out


def _ssd_kernel(cumc_ref, dtc_ref, cumr_ref, z_ref, x_ref, b_ref, c_ref,
                wx_ref, wb_ref, wc_ref, bx_ref, bb_ref, bc_ref, ux_ref, ub_ref, uc_ref,
                d_ref, ng_ref, h0_ref,
                y_ref, hout_ref,
                hs_ref, catx_ref, catb_ref, catc_ref, ybuf_ref, *, q, n_pairs, n_chunks):
    c_idx = pl.program_id(2)
    p2 = 2 * SSD_HEAD_DIM

    @pl.when(c_idx == 0)
    def _():
        hs_ref[...] = h0_ref[0]
        _conv_init(catx_ref, ux_ref)
        _conv_init(catb_ref, ub_ref)
        _conv_init(catc_ref, uc_ref)

    xs = _silu(_conv_window(catx_ref, x_ref[...], wx_ref, q) + bx_ref[...])
    bm = _silu(_conv_window(catb_ref, b_ref[...], wb_ref, q) + bb_ref[...])
    cm = _silu(_conv_window(catc_ref, c_ref[...], wc_ref, q) + bc_ref[...])

    cumc = cumc_ref[0, 0]
    dtc = dtc_ref[0, 0]
    cumr = cumr_ref[0, 0]
    cb = _dot_nt(cm, bm)
    rr = lax.broadcasted_iota(jnp.int32, (q, q), 0)
    cc = lax.broadcasted_iota(jnp.int32, (q, q), 1)
    tril = rr >= cc
    lo = lax.broadcasted_iota(jnp.int32, (q, p2), 1) < SSD_HEAD_DIM
    lo_row = lax.broadcasted_iota(jnp.int32, (1, p2), 1) < SSD_HEAD_DIM
    top = lax.broadcasted_iota(jnp.int32, (p2, 1), 0) < SSD_HEAD_DIM
    cm_b = cm.astype(BF16)
    bm_b = bm.astype(BF16)

    for pair in range(n_pairs):
        ra, rb = 2 * pair, 2 * pair + 1
        ca, cbk = cumc[:, ra:ra + 1], cumc[:, rb:rb + 1]
        cum_pair = jnp.where(lo, ca, cbk)
        dt_pair = jnp.where(lo, dtc[:, ra:ra + 1], dtc[:, rb:rb + 1])
        last_a = cumr[ra:ra + 1, q - 1:q]
        last_b = cumr[rb:rb + 1, q - 1:q]
        last_pair = jnp.where(lo_row, last_a, last_b)
        xdt = xs[:, pair * p2:(pair + 1) * p2] * dt_pair
        xw = xdt * jnp.exp(last_pair - cum_pair)
        m_a = cb * jnp.exp(jnp.where(tril, ca - cumr[ra:ra + 1, :], -jnp.inf))
        m_b = cb * jnp.exp(jnp.where(tril, cbk - cumr[rb:rb + 1, :], -jnp.inf))
        y = _dot(m_a, jnp.where(lo, xdt, 0.0)) + _dot(m_b, jnp.where(lo, 0.0, xdt))
        hp = hs_ref[ra:ra + 2].reshape(p2, SSD_STATE)
        y = y + _dot_nt(cm_b, hp) * jnp.exp(cum_pair)
        ybuf_ref[:, pair * p2:(pair + 1) * p2] = y
        scale = jnp.where(top, jnp.exp(last_a), jnp.exp(last_b))
        hn = hp * scale + _dot(xw.T, bm_b)
        hs_ref[ra:ra + 2] = hn.reshape(2, SSD_HEAD_DIM, SSD_STATE)

    yf = (ybuf_ref[...] + xs * d_ref[...]) * _silu(z_ref[...])
    ms = jnp.mean(yf * yf, axis=-1, keepdims=True)
    y_ref[...] = (yf * lax.rsqrt(ms + RMS_EPS) * ng_ref[...]).astype(BF16)

    @pl.when(c_idx == n_chunks - 1)
    def _():
        hout_ref[0] = hs_ref[...]


def _ssd_scan(proj, cumc, dtc, cumr, conv_w, conv_b, conv_buf, d_exp, norm_g, h0,
              *, row0, bsz, seq, q, inner):
    g_n, n = SSD_GROUPS, SSD_STATE
    gw = inner // g_n
    r = gw // SSD_HEAD_DIM
    nc = seq // q
    c0 = row0 // q
    xb = inner // gw
    bb = 2 * inner // n
    cvb = inner // n

    def rowblk(s, g, c):
        return c0 + s * nc + c

    in_specs = [
        pl.BlockSpec((1, 1, q, r), lambda s, g, c: (g, s * nc + c, 0, 0)),
        pl.BlockSpec((1, 1, q, r), lambda s, g, c: (g, s * nc + c, 0, 0)),
        pl.BlockSpec((1, 1, r, q), lambda s, g, c: (g, s * nc + c, 0, 0)),
        pl.BlockSpec((q, gw), lambda s, g, c: (rowblk(s, g, c), g)),
        pl.BlockSpec((q, gw), lambda s, g, c: (rowblk(s, g, c), xb + g)),
        pl.BlockSpec((q, n), lambda s, g, c: (rowblk(s, g, c), bb + g)),
        pl.BlockSpec((q, n), lambda s, g, c: (rowblk(s, g, c), bb + g_n + g)),
        pl.BlockSpec((CONV_W, gw), lambda s, g, c: (0, g)),
        pl.BlockSpec((CONV_W, n), lambda s, g, c: (0, cvb + g)),
        pl.BlockSpec((CONV_W, n), lambda s, g, c: (0, cvb + g_n + g)),
        pl.BlockSpec((1, gw), lambda s, g, c: (0, g)),
        pl.BlockSpec((1, n), lambda s, g, c: (0, cvb + g)),
        pl.BlockSpec((1, n), lambda s, g, c: (0, cvb + g_n + g)),
        pl.BlockSpec((1, CONV_W - 1, gw), lambda s, g, c: (s, 0, g)),
        pl.BlockSpec((1, CONV_W - 1, n), lambda s, g, c: (s, 0, cvb + g)),
        pl.BlockSpec((1, CONV_W - 1, n), lambda s, g, c: (s, 0, cvb + g_n + g)),
        pl.BlockSpec((1, gw), lambda s, g, c: (0, g)),
        pl.BlockSpec((1, gw), lambda s, g, c: (0, g)),
        pl.BlockSpec((1, r, SSD_HEAD_DIM, n), lambda s, g, c: (s, g, 0, 0)),
    ]
    out_specs = [
        pl.BlockSpec((q, gw), lambda s, g, c: (s * nc + c, g)),
        pl.BlockSpec((1, r, SSD_HEAD_DIM, n), lambda s, g, c: (s, g, 0, 0)),
    ]
    out_shape = [jax.ShapeDtypeStruct((bsz * seq, inner), BF16),
                 jax.ShapeDtypeStruct(h0.shape, F32)]
    scratch = [pltpu.VMEM((r, SSD_HEAD_DIM, n), F32),
               pltpu.VMEM((q + CONV_TAIL, gw), F32),
               pltpu.VMEM((q + CONV_TAIL, n), F32),
               pltpu.VMEM((q + CONV_TAIL, n), F32),
               pltpu.VMEM((q, gw), F32)]
    vmem = q * gw * 4 * 12 + q * q * 4 * 8
    cb2 = conv_b.reshape(1, -1)
    return pl.pallas_call(
        functools.partial(_ssd_kernel, q=q, n_pairs=r // 2, n_chunks=nc),
        grid=(bsz, g_n, nc),
        in_specs=in_specs, out_specs=out_specs, out_shape=out_shape, scratch_shapes=scratch,
        compiler_params=_params(("arbitrary", "arbitrary", "arbitrary"), vmem),
    )(cumc, dtc, cumr, proj, proj, proj, proj, conv_w, conv_w, conv_w, cb2, cb2, cb2,
      conv_buf, conv_buf, conv_buf, d_exp, norm_g.reshape(1, -1), h0)


def _gdn_kernel(col_ref, row_ref, q_ref, k_ref, v_ref, z_ref, wq_ref, wk_ref, wv_ref,
                uq_ref, uk_ref, uv_ref, ng_ref, s0_ref,
                o_ref, sout_ref,
                s_ref, catq_ref, catk_ref, catv_ref, *, cps, n_steps, rep, hpb):
    c = GDN_CHUNK
    dh = GDN_HEAD_DIM
    vw = rep * dh
    rows = cps * c
    step = pl.program_id(2)

    @pl.when(step == 0)
    def _():
        s_ref[...] = s0_ref[0]
        _conv_init(catq_ref, uq_ref)
        _conv_init(catk_ref, uk_ref)
        _conv_init(catv_ref, uv_ref)

    q_all = _silu(_conv_window(catq_ref, q_ref[...], wq_ref, rows))
    k_all = _silu(_conv_window(catk_ref, k_ref[...], wk_ref, rows))
    v_all = _silu(_conv_window(catv_ref, v_ref[...], wv_ref, rows))

    nb = cps * rep
    m = nb * c
    ri = lax.broadcasted_iota(jnp.int32, (m, m), 0)
    ci = lax.broadcasted_iota(jnp.int32, (m, m), 1)
    shift = c.bit_length() - 1
    same = jnp.right_shift(ri, shift) == jnp.right_shift(ci, shift)
    incl = same & (ri >= ci)
    strict = same & (ri > ci)
    eye = (ri == ci).astype(F32)
    steps = max(1, int(math.ceil(math.log2(c))) - 1)

    s_all = [s_ref[i] for i in range(hpb * rep)]
    heads = range(hpb)
    ng = ng_ref[...]

    def l2n(a):
        return a * lax.rsqrt(jnp.sum(a * a, axis=-1, keepdims=True) + L2_EPS)

    qa = [l2n(q_all[:, kh * dh:(kh + 1) * dh]) * (dh ** -0.5) for kh in heads]
    ka = [l2n(k_all[:, kh * dh:(kh + 1) * dh]) for kh in heads]
    cols = [[col_ref[kh, ch] for ch in range(cps)] for kh in heads]
    rws = [[row_ref[kh, ch] for ch in range(cps)] for kh in heads]

    def stack_rows(a):
        return jnp.concatenate([a[(i // rep) * c:(i // rep + 1) * c] for i in range(nb)], axis=0)

    k_big = [stack_rows(ka[kh]) for kh in heads]
    q_big = [stack_rows(qa[kh]) for kh in heads]
    v_big = [jnp.concatenate(
        [v_all[(i // rep) * c:(i // rep + 1) * c, (kh * rep + i % rep) * dh:(kh * rep + i % rep + 1) * dh]
         for i in range(nb)], axis=0) for kh in heads]
    gcol = [jnp.concatenate([cols[kh][i // rep][:, (i % rep):(i % rep) + 1] for i in range(nb)], axis=0)
            for kh in heads]
    bcol = [jnp.concatenate([cols[kh][i // rep][:, rep + (i % rep):rep + (i % rep) + 1] for i in range(nb)],
                            axis=0) for kh in heads]
    grow = [jnp.concatenate([rws[kh][i // rep][(i % rep):(i % rep) + 1, :] for i in range(nb)], axis=1)
            for kh in heads]
    glast = [jnp.concatenate(
        [jnp.broadcast_to(rws[kh][i // rep][(i % rep):(i % rep) + 1, c - 1:c], (c, 1)) for i in range(nb)],
        axis=0) for kh in heads]

    decay = [jnp.exp(jnp.where(incl, gcol[kh] - grow[kh], -jnp.inf)) for kh in heads]
    kk = [_dot_nt(k_big[kh], k_big[kh]) for kh in heads]
    a_mat = [jnp.where(strict, kk[kh] * decay[kh] * bcol[kh], 0.0) for kh in heads]
    t_inv = [eye - a_mat[kh] for kh in heads]
    pw = a_mat
    for _ in range(steps):
        pw = [_dot(pw[kh], pw[kh]) for kh in heads]
        t_inv = [t_inv[kh] + _dot(t_inv[kh], pw[kh]) for kh in heads]
    egc = [jnp.exp(gcol[kh]) for kh in heads]
    rhs = [jnp.concatenate([v_big[kh] * bcol[kh], k_big[kh] * (bcol[kh] * egc[kh])], axis=1) for kh in heads]
    sol = [_dot(t_inv[kh], rhs[kh]) for kh in heads]
    qk = [_dot_nt(q_big[kh], k_big[kh]) * decay[kh] for kh in heads]
    qe = [q_big[kh] * egc[kh] for kh in heads]
    kd = [k_big[kh] * jnp.exp(glast[kh] - gcol[kh]) for kh in heads]
    eg_last = [jnp.exp(glast[kh]) for kh in heads]

    chains = [(kh, h) for kh in heads for h in range(rep)]
    for ch in range(cps):
        def blk(kh, h):
            return slice((ch * rep + h) * c, (ch * rep + h + 1) * c)
        wq_s = [_dot(jnp.concatenate([sol[kh][blk(kh, h), dh:], qe[kh][blk(kh, h)]], axis=0),
                     s_all[kh * rep + h]) for kh, h in chains]
        v_new = [sol[kh][blk(kh, h), :dh] - wq_s[j][:c] for j, (kh, h) in enumerate(chains)]
        o = [wq_s[j][c:] + _dot(qk[kh][blk(kh, h), blk(kh, h)], v_new[j]) for j, (kh, h) in enumerate(chains)]
        upd = [_dot(kd[kh][blk(kh, h)].T, v_new[j]) for j, (kh, h) in enumerate(chains)]
        for j, (kh, h) in enumerate(chains):
            i0 = (ch * rep + h) * c
            s_all[kh * rep + h] = s_all[kh * rep + h] * eg_last[kh][i0:i0 + 1, :] + upd[j]
        o = [v * lax.rsqrt(jnp.mean(v * v, axis=-1, keepdims=True) + RMS_EPS) * ng for v in o]
        zc = _silu(z_ref[ch * c:(ch + 1) * c, :])
        o_ref[ch * c:(ch + 1) * c, :] = (jnp.concatenate(o, axis=1) * zc).astype(BF16)

    for i in range(hpb * rep):
        s_ref[i] = s_all[i]

    @pl.when(step == n_steps - 1)
    def _():
        sout_ref[0] = s_ref[...]


def _gdn_scan(proj, colf, rowf, conv_w, conv_buf, norm_g, s0, *, row0, bsz, seq, cps, hpb, key_dim, val_dim):
    dh, c = GDN_HEAD_DIM, GDN_CHUNK
    hk = key_dim // dh
    rep = val_dim // key_dim
    kw = hpb * dh
    vw = hpb * rep * dh
    rows = cps * c
    n_steps = seq // rows
    r0 = row0 // rows
    kb = key_dim // kw
    vb = 2 * key_dim // vw
    zb = (2 * key_dim + val_dim) // vw

    def rowblk(s, h, t):
        return r0 + s * n_steps + t

    in_specs = [
        pl.BlockSpec((hpb, cps, c, 2 * rep), lambda s, h, t: (h, rowblk(s, h, t), 0, 0)),
        pl.BlockSpec((hpb, cps, 2 * rep, c), lambda s, h, t: (h, rowblk(s, h, t), 0, 0)),
        pl.BlockSpec((rows, kw), lambda s, h, t: (rowblk(s, h, t), h)),
        pl.BlockSpec((rows, kw), lambda s, h, t: (rowblk(s, h, t), kb + h)),
        pl.BlockSpec((rows, vw), lambda s, h, t: (rowblk(s, h, t), vb + h)),
        pl.BlockSpec((rows, vw), lambda s, h, t: (rowblk(s, h, t), zb + h)),
        pl.BlockSpec((CONV_W, kw), lambda s, h, t: (0, h)),
        pl.BlockSpec((CONV_W, kw), lambda s, h, t: (0, kb + h)),
        pl.BlockSpec((CONV_W, vw), lambda s, h, t: (0, vb + h)),
        pl.BlockSpec((1, CONV_W - 1, kw), lambda s, h, t: (s, 0, h)),
        pl.BlockSpec((1, CONV_W - 1, kw), lambda s, h, t: (s, 0, kb + h)),
        pl.BlockSpec((1, CONV_W - 1, vw), lambda s, h, t: (s, 0, vb + h)),
        pl.BlockSpec((1, dh), lambda s, h, t: (0, 0)),
        pl.BlockSpec((1, hpb * rep, dh, dh), lambda s, h, t: (s, h, 0, 0)),
    ]
    out_specs = [
        pl.BlockSpec((rows, vw), lambda s, h, t: (s * n_steps + t, h)),
        pl.BlockSpec((1, hpb * rep, dh, dh), lambda s, h, t: (s, h, 0, 0)),
    ]
    out_shape = [jax.ShapeDtypeStruct((bsz * seq, val_dim), BF16),
                 jax.ShapeDtypeStruct(s0.shape, F32)]
    scratch = [pltpu.VMEM((hpb * rep, dh, dh), F32),
               pltpu.VMEM((rows + CONV_TAIL, kw), F32),
               pltpu.VMEM((rows + CONV_TAIL, kw), F32),
               pltpu.VMEM((rows + CONV_TAIL, vw), F32)]
    m = cps * rep * c
    vmem = rows * (2 * kw + 2 * vw) * 4 * 6 + hpb * m * m * 4 * 12
    return pl.pallas_call(
        functools.partial(_gdn_kernel, cps=cps, n_steps=n_steps, rep=rep, hpb=hpb),
        grid=(bsz, hk // hpb, n_steps),
        in_specs=in_specs, out_specs=out_specs, out_shape=out_shape, scratch_shapes=scratch,
        compiler_params=_params(("arbitrary", "arbitrary", "arbitrary"), vmem),
    )(colf, rowf, proj, proj, proj, proj, conv_w, conv_w, conv_w,
      conv_buf, conv_buf, conv_buf, norm_g.reshape(1, dh), s0)


def _split_bf16(a):
    hi = a.astype(BF16)
    lo = (a - hi.astype(F32)).astype(BF16)
    return hi, lo


def _router_kernel(x_ref, wt_ref, b_ref, e_ref, gate_ref, rank_ref, cnt_ref, carry_ref, *, n_steps):
    i = pl.program_id(0)
    n_e = wt_ref.shape[0]
    tr = x_ref.shape[0]

    @pl.when(i == 0)
    def _():
        carry_ref[...] = jnp.zeros_like(carry_ref)

    xh, xl = _split_bf16(x_ref[...])
    wh, wl = _split_bf16(wt_ref[...])
    nt = (((1,), (1,)), ((), ()))
    logits = (lax.dot_general(wh, xh, nt, preferred_element_type=F32)
              + lax.dot_general(wh, xl, nt, preferred_element_type=F32)
              + lax.dot_general(wl, xh, nt, preferred_element_type=F32)) + b_ref[...]

    eidx = lax.broadcasted_iota(jnp.int32, (n_e, tr), 0)
    work = logits
    tops, idxs = [], []
    for _ in range(TOP_K):
        mval = jnp.max(work, axis=0, keepdims=True)
        idx = jnp.min(jnp.where(work == mval, eidx, n_e), axis=0, keepdims=True)
        tops.append(mval)
        idxs.append(idx)
        work = jnp.where(eidx == idx, -jnp.inf, work)
    exps = [jnp.exp(t - tops[0]) for t in tops]
    den = exps[0]
    for e in exps[1:]:
        den = den + e

    onehot = jnp.zeros((n_e, tr), F32)
    for idx in idxs:
        onehot = onehot + (eidx == idx).astype(F32)
    rr = lax.broadcasted_iota(jnp.int32, (tr, tr), 0)
    cc = lax.broadcasted_iota(jnp.int32, (tr, tr), 1)
    upper = (rr <= cc).astype(BF16)
    incl = jnp.dot(onehot.astype(BF16), upper, preferred_element_type=F32)
    before = carry_ref[...][:, 0:1] + incl - onehot
    for k in range(TOP_K):
        e_ref[k:k + 1, :] = idxs[k]
        gate_ref[k:k + 1, :] = exps[k] / den
        rank = jnp.sum(jnp.where(eidx == idxs[k], before, 0.0), axis=0, keepdims=True)
        rank_ref[k:k + 1, :] = rank.astype(jnp.int32)
    carry_ref[...] = carry_ref[...] + incl[:, tr - 1:tr]

    @pl.when(i == n_steps - 1)
    def _():
        cnt_ref[...] = carry_ref[...].astype(jnp.int32)


def _router(x, w_router, b_router):
    t, d = x.shape
    n_e = w_router.shape[1]
    tr = _tile(t, 512, LANES)
    n_steps = t // tr
    tok = pl.BlockSpec((TOP_K, tr), lambda i: (0, i))
    return pl.pallas_call(
        functools.partial(_router_kernel, n_steps=n_steps),
        grid=(n_steps,),
        in_specs=[pl.BlockSpec((tr, d), lambda i: (i, 0)),
                  pl.BlockSpec((n_e, d), lambda i: (0, 0)),
                  pl.BlockSpec((n_e, 1), lambda i: (0, 0))],
        out_specs=[tok, tok, tok, pl.BlockSpec((n_e, LANES), lambda i: (0, 0))],
        out_shape=[jax.ShapeDtypeStruct((TOP_K, t), jnp.int32),
                   jax.ShapeDtypeStruct((TOP_K, t), F32),
                   jax.ShapeDtypeStruct((TOP_K, t), jnp.int32),
                   jax.ShapeDtypeStruct((n_e, LANES), jnp.int32)],
        scratch_shapes=[pltpu.VMEM((n_e, LANES), F32)],
        compiler_params=_params(("arbitrary",), tr * d * 4 * 4 + tr * tr * 8),
    )(x, w_router.T, b_router.reshape(n_e, 1))


def _dispatch_kernel(tok_ref, used_ref, x_ref, xs_ref, buf_ref, sem, *, tg, n_steps):
    i = pl.program_id(0)
    slot = i % 2
    n_used = used_ref[0]

    def issue(step, s):
        def body(r, carry):
            src = tok_ref[step * tg + r]
            pltpu.make_async_copy(x_ref.at[pl.ds(src, 1)], buf_ref.at[s, pl.ds(r, 1)], sem.at[s]).start()
            return carry
        lax.fori_loop(0, tg, body, 0, unroll=8)

    @pl.when((i == 0) & (n_used > 0))
    def _():
        issue(0, 0)

    @pl.when((i + 1 < n_steps) & ((i + 1) * tg < n_used))
    def _():
        issue(i + 1, 1 - slot)

    @pl.when(i * tg < n_used)
    def _():
        def drain(r, carry):
            pltpu.make_async_copy(x_ref.at[pl.ds(0, 1)], buf_ref.at[slot, pl.ds(r, 1)], sem.at[slot]).wait()
            return carry
        lax.fori_loop(0, tg, drain, 0, unroll=8)
        xs_ref[...] = buf_ref[slot].astype(BF16)

    @pl.when(i * tg >= n_used)
    def _():
        xs_ref[...] = jnp.zeros_like(xs_ref)


def _dispatch(x, row_tok, n_used):
    t, d = x.shape
    n_rows = row_tok.shape[0]
    tg = _tile(n_rows, 256, 16)
    n_steps = n_rows // tg
    grid_spec = pltpu.PrefetchScalarGridSpec(
        num_scalar_prefetch=2,
        grid=(n_steps,),
        in_specs=[pl.BlockSpec(memory_space=pl.ANY)],
        out_specs=pl.BlockSpec((tg, d), lambda i, tok, used: (i, 0)),
        scratch_shapes=[pltpu.VMEM((2, tg, d), F32), pltpu.SemaphoreType.DMA((2,))],
    )
    return pl.pallas_call(
        functools.partial(_dispatch_kernel, tg=tg, n_steps=n_steps),
        grid_spec=grid_spec,
        out_shape=jax.ShapeDtypeStruct((n_rows, d), BF16),
        compiler_params=_params(("arbitrary",), 2 * tg * d * 4 + 3 * tg * d * 2),
    )(row_tok, n_used, x)


def _cast_rows(dst_ref, src_ref, k_rows, k_step):
    def body(c, carry):
        r = pl.multiple_of(c * k_step, k_step)
        dst_ref[pl.ds(r, k_step), :] = src_ref[0, pl.ds(r, k_step), :].astype(BF16)
        return carry
    lax.fori_loop(0, k_rows // k_step, body, 0)


def _is_new_expert(be_ref, blk):
    prev = be_ref[jnp.maximum(blk - 1, 0)]
    return (blk == 0) | (be_ref[blk] != prev)


def _row_steps(tm):
    quarter = tm // 4
    if tm % 4 == 0 and quarter % 16 == 0:
        return tuple(quarter * i for i in range(1, 5))
    return (tm,)


def _for_filled_rows(rv_ref, blk, o_ref, compute):
    tm = o_ref.shape[0]
    filled = rv_ref[blk]
    lo = 0
    for r in _row_steps(tm):
        @pl.when((filled > lo) & (filled <= r))
        def _(r=r):
            o_ref[0:r, :] = compute(r)
            if r < tm:
                o_ref[r:tm, :] = jnp.zeros((tm - r, o_ref.shape[1]), o_ref.dtype)
        lo = r

    @pl.when(filled == 0)
    def _():
        o_ref[...] = jnp.zeros_like(o_ref)


def _gate_up_kernel(be_ref, rv_ref, x_ref, wg_ref, wu_ref, bg_ref, bu_ref, o_ref, wgb_ref, wub_ref,
                    *, k_rows, k_step):
    blk = pl.program_id(1)

    @pl.when(_is_new_expert(be_ref, blk))
    def _():
        _cast_rows(wgb_ref, wg_ref, k_rows, k_step)
        _cast_rows(wub_ref, wu_ref, k_rows, k_step)

    def compute(r):
        cuts = list(range(0, r, MOE_ROW_PIECE)) + [r]
        pieces = [x_ref[a:b, :] for a, b in zip(cuts[:-1], cuts[1:])]
        hs = [(jnp.dot(xb, wgb_ref[...], preferred_element_type=F32),
               jnp.dot(xb, wub_ref[...], preferred_element_type=F32)) for xb in pieces]
        outs = []
        for hg, hu in hs:
            hg = jnp.minimum(hg + bg_ref[0], SWIGLU_LIMIT)
            hu = jnp.clip(hu + bu_ref[0], -SWIGLU_LIMIT, SWIGLU_LIMIT)
            outs.append(((hu + 1.0) * hg * jax.nn.sigmoid(SWIGLU_ALPHA * hg)).astype(BF16))
        return outs[0] if len(outs) == 1 else jnp.concatenate(outs, axis=0)

    _for_filled_rows(rv_ref, blk, o_ref, compute)


def _moe_gate_up(xs, w_gate_up, b_gate_up, layer, block_e, rows_filled, tm, tn_pref=512):
    n_rows, d = xs.shape
    n_e = w_gate_up.shape[1]
    f = w_gate_up.shape[3] // 2
    tn = _tile(f, tn_pref, LANES)
    nj = f // tn
    k_step = _tile(d, 256, 16)
    w4 = w_gate_up
    b4 = b_gate_up.reshape(b_gate_up.shape[0] * n_e, 1, 2 * f)
    w3 = w4.reshape(w4.shape[0] * n_e, d, 2 * f)
    e0 = layer * n_e
    grid_spec = pltpu.PrefetchScalarGridSpec(
        num_scalar_prefetch=2,
        grid=(nj, n_rows // tm),
        in_specs=[
            pl.BlockSpec((tm, d), lambda j, b, be, nv: (b, 0)),
            pl.BlockSpec((1, d, tn), lambda j, b, be, nv: (e0 + be[b], 0, j)),
            pl.BlockSpec((1, d, tn), lambda j, b, be, nv: (e0 + be[b], 0, nj + j)),
            pl.BlockSpec((1, 1, tn), lambda j, b, be, nv: (e0 + be[b], 0, j)),
            pl.BlockSpec((1, 1, tn), lambda j, b, be, nv: (e0 + be[b], 0, nj + j)),
        ],
        out_specs=pl.BlockSpec((tm, tn), lambda j, b, be, nv: (b, j)),
        scratch_shapes=[pltpu.VMEM((d, tn), BF16), pltpu.VMEM((d, tn), BF16)],
    )
    vmem = tm * d * 2 * 2 + d * tn * 4 * 4 + d * tn * 2 * 2 + tm * tn * 2 * 2 + tm * tn * 4 * 4
    return pl.pallas_call(
        functools.partial(_gate_up_kernel, k_rows=d, k_step=k_step),
        grid_spec=grid_spec,
        out_shape=jax.ShapeDtypeStruct((n_rows, f), BF16),
        compiler_params=_params(("arbitrary", "arbitrary"), vmem),
    )(block_e, rows_filled, xs, w3, w3, b4, b4)


def _down_kernel(be_ref, rv_ref, a_ref, w_ref, b_ref, o_ref, wb_ref, *, k_rows, k_step):
    blk = pl.program_id(1)

    @pl.when(_is_new_expert(be_ref, blk))
    def _():
        _cast_rows(wb_ref, w_ref, k_rows, k_step)

    def compute(r):
        return jnp.dot(a_ref[0:r, :], wb_ref[...], preferred_element_type=F32) + b_ref[0]

    _for_filled_rows(rv_ref, blk, o_ref, compute)


def _moe_down(act, w_down, b_down, layer, block_e, rows_filled, tm, tn_pref=512):
    n_rows, f = act.shape
    n_e = w_down.shape[1]
    d = w_down.shape[3]
    tn = _tile(d, tn_pref, LANES)
    k_step = _tile(f, 256, 16)
    w3 = w_down.reshape(w_down.shape[0] * n_e, f, d)
    b3 = b_down.reshape(b_down.shape[0] * n_e, 1, d)
    e0 = layer * n_e
    grid_spec = pltpu.PrefetchScalarGridSpec(
        num_scalar_prefetch=2,
        grid=(d // tn, n_rows // tm),
        in_specs=[
            pl.BlockSpec((tm, f), lambda j, b, be, nv: (b, 0)),
            pl.BlockSpec((1, f, tn), lambda j, b, be, nv: (e0 + be[b], 0, j)),
            pl.BlockSpec((1, 1, tn), lambda j, b, be, nv: (e0 + be[b], 0, j)),
        ],
        out_specs=pl.BlockSpec((tm, tn), lambda j, b, be, nv: (b, j)),
        scratch_shapes=[pltpu.VMEM((f, tn), BF16)],
    )
    vmem = tm * f * 2 * 2 + f * tn * 4 * 2 + f * tn * 2 + tm * tn * 4 * 3
    return pl.pallas_call(
        functools.partial(_down_kernel, k_rows=f, k_step=k_step),
        grid_spec=grid_spec,
        out_shape=jax.ShapeDtypeStruct((n_rows, d), F32),
        compiler_params=_params(("arbitrary", "arbitrary"), vmem),
    )(block_e, rows_filled, act, w3, b3)


def _combine_kernel(pos_ref, y_ref, x_ref, gate_ref, g_ref, b_ref, o_ref, ob_ref, buf_ref, sem,
                    *, tt, n_steps, alpha):
    i = pl.program_id(0)
    slot = i % 2

    def row_copy(step, t, k, s):
        src = pos_ref[(step * tt + t) * TOP_K + k]
        return pltpu.make_async_copy(y_ref.at[pl.ds(src, 1)], buf_ref.at[s, k, pl.ds(t, 1)], sem.at[s])

    def issue(step, s):
        def body(t, carry):
            for k in range(TOP_K):
                row_copy(step, t, k, s).start()
            return carry
        lax.fori_loop(0, tt, body, 0)

    @pl.when(i == 0)
    def _():
        issue(0, 0)

    @pl.when(i + 1 < n_steps)
    def _():
        issue(i + 1, 1 - slot)

    def drain(t, carry):
        for k in range(TOP_K):
            pltpu.make_async_copy(y_ref.at[pl.ds(0, 1)], buf_ref.at[slot, k, pl.ds(t, 1)], sem.at[slot]).wait()
        return carry
    lax.fori_loop(0, tt, drain, 0)

    gates = gate_ref[...]
    acc = alpha * x_ref[...]
    for k in range(TOP_K):
        acc = acc + buf_ref[slot, k] * gates[:, k:k + 1]
    out = _ln_rows(acc, g_ref[...], b_ref[...])
    o_ref[...] = out
    ob_ref[...] = out.astype(BF16)


def _moe_combine_ln(y_rows, pos_flat, gates_t, x, g, b, alpha):
    t, d = x.shape
    tt = _tile(t, 64, 16)
    n_steps = t // tt
    row = pl.BlockSpec((tt, d), lambda i, pos: (i, 0))
    vec = pl.BlockSpec((1, d), lambda i, pos: (0, 0))
    grid_spec = pltpu.PrefetchScalarGridSpec(
        num_scalar_prefetch=1,
        grid=(n_steps,),
        in_specs=[pl.BlockSpec(memory_space=pl.ANY), row,
                  pl.BlockSpec((tt, TOP_K), lambda i, pos: (i, 0)), vec, vec],
        out_specs=[row, row],
        scratch_shapes=[pltpu.VMEM((2, TOP_K, tt, d), F32), pltpu.SemaphoreType.DMA((2,))],
    )
    vmem = 2 * TOP_K * tt * d * 4 + tt * d * 4 * 8
    return pl.pallas_call(
        functools.partial(_combine_kernel, tt=tt, n_steps=n_steps, alpha=alpha),
        grid_spec=grid_spec,
        out_shape=[jax.ShapeDtypeStruct((t, d), F32), jax.ShapeDtypeStruct((t, d), BF16)],
        compiler_params=_params(("arbitrary",), vmem),
    )(pos_flat, y_rows, x, gates_t, g.reshape(1, d), b.reshape(1, d))


def _moe_layer(x, layer, w_router, b_router, w_gate_up, b_gate_up, w_down, b_down, ln_g, ln_b, alpha,
               tm_pref=512):
    t, d = x.shape
    n_e = w_router.shape[2]
    tm = _tile(t, tm_pref, 16)
    top_e, gates, rank, cnt = _router(x, w_router[layer], b_router[layer])
    counts = cnt[:, 0]
    padded = (counts + tm - 1) // tm * tm
    pad_end = jnp.cumsum(padded)
    pad_start = pad_end - padded
    n_blocks = (t * TOP_K) // tm + n_e
    n_rows = n_blocks * tm
    start_of = jnp.sum(jnp.where(top_e[..., None] == jnp.arange(n_e, dtype=jnp.int32), pad_start, 0), axis=-1)
    pos = (start_of + rank).T.reshape(-1).astype(jnp.int32)
    row_tok = jnp.zeros((n_rows,), jnp.int32).at[pos].set(
        jnp.arange(t * TOP_K, dtype=jnp.int32) // TOP_K, unique_indices=True)
    blk_row0 = jnp.arange(n_blocks, dtype=jnp.int32) * tm
    last_row = jnp.minimum(blk_row0, pad_end[-1] - 1)
    block_e = jnp.sum(pad_end[None, :] <= last_row[:, None], axis=1)
    block_e = jnp.minimum(block_e, n_e - 1).astype(jnp.int32)
    rows_filled = jnp.clip(pad_start[block_e] + counts[block_e] - blk_row0, 0, tm).astype(jnp.int32)

    xs = _dispatch(x, row_tok, pad_end[-1:].astype(jnp.int32))
    act = _moe_gate_up(xs, w_gate_up, b_gate_up, layer, block_e, rows_filled, tm)
    y_rows = _moe_down(act, w_down, b_down, layer, block_e, rows_filled, tm)
    return _moe_combine_ln(y_rows, pos, gates.T, x, ln_g[layer, 1], ln_b[layer, 1], alpha)


def _last_rows(proj, row0, bsz, seq, col0, col1):
    ends = [row0 + (s + 1) * seq for s in range(bsz)]
    return jnp.stack([proj[e - (CONV_W - 1):e, col0:col1] for e in ends])


def _per_head_forms(col_arrays, n_groups, q):
    t = col_arrays[0].shape[0]
    parts = [a.reshape(t // q, q, n_groups, -1) for a in col_arrays]
    colf = jnp.concatenate(parts, axis=-1).transpose(2, 0, 1, 3)
    return colf, colf.transpose(0, 1, 3, 2)


def _ssd_layer(xb, groups, state, conv_cache, w_in, conv_w, conv_b, dt_bias, a_log, d_skip, norm_g, w_out):
    inner = w_out.shape[0]
    heads = inner // SSD_HEAD_DIM
    gn = SSD_GROUPS * SSD_STATE
    main_cols = 2 * inner + 2 * gn
    proj = _matmul(xb, w_in, 0, main_cols, tm_pref=768, tn_pref=1024, w_buffers=1)
    dt_raw = _matmul(xb, w_in[:, main_cols:], 0, heads)
    d_exp = jnp.repeat(d_skip, SSD_HEAD_DIM).reshape(1, inner)
    ys, hs, caches = [], [], []
    for (row0, bsz, seq, q), h0, cbuf in zip(groups, state, conv_cache):
        dt, cum = _ssd_prep(dt_raw, dt_bias, a_log, q, row0, bsz * seq)
        colf, rowf = _per_head_forms([cum, dt], SSD_GROUPS, q)
        r = heads // SSD_GROUPS
        y, h = _ssd_scan(proj, colf[..., :r], colf[..., r:], rowf[:, :, :r], conv_w, conv_b, cbuf,
                         d_exp, norm_g, h0, row0=row0, bsz=bsz, seq=seq, q=q, inner=inner)
        ys.append(y)
        hs.append(h)
        caches.append(_last_rows(proj, row0, bsz, seq, inner, 2 * inner + 2 * gn))
    mix = _matmul(jnp.concatenate(ys, axis=0), w_out, 0, w_out.shape[1], w_buffers=1)
    return mix, hs, caches


def _gdn_layer(xb, groups, state, conv_cache, w_in, conv_w, a_log, dt_bias, norm_g, w_out):
    val_dim = w_out.shape[0]
    hv = a_log.shape[0]
    key_dim = (conv_w.shape[1] - val_dim) // 2
    hk = key_dim // GDN_HEAD_DIM
    main_cols = 2 * key_dim + 2 * val_dim
    proj = _matmul(xb, w_in, 0, main_cols, tm_pref=768, tn_pref=1024, w_buffers=1)
    ba = _matmul(xb, w_in[:, main_cols:], 0, 2 * hv)
    beta, gc = _gdn_prep(ba, a_log, dt_bias, GDN_CHUNK)
    colf, rowf = _per_head_forms([gc, beta], hk, GDN_CHUNK)
    os_, ss, caches = [], [], []
    for (row0, bsz, seq, cps), s0, cbuf in zip(groups, state, conv_cache):
        o, s = _gdn_scan(proj, colf, rowf, conv_w, cbuf, norm_g, s0, row0=row0, bsz=bsz, seq=seq,
                         cps=cps, hpb=_tile(hk, GDN_HEADS_PER_PROGRAM, 1), key_dim=key_dim, val_dim=val_dim)
        os_.append(o)
        ss.append(s)
        caches.append(_last_rows(proj, row0, bsz, seq, 0, 2 * key_dim + val_dim))
    mix = _matmul(jnp.concatenate(os_, axis=0), w_out, 0, w_out.shape[1], w_buffers=1)
    return mix, ss, caches


def kernel(x_prompt, x_sample, state_ssd, cache_ssd_conv, state_gdn, cache_gdn_conv, ssd_w_in, ssd_conv_w, ssd_conv_b, ssd_dt_bias, ssd_a_log, ssd_d, ssd_norm_g, ssd_w_out, gdn_w_in, gdn_conv_w, gdn_a_log, gdn_dt_bias, gdn_norm_g, gdn_w_out, moe_w_router, moe_b_router, moe_w_gate_up, moe_b_gate_up, moe_w_down, moe_b_down, ln_g, ln_b):
    bp, lp, d = x_prompt.shape
    bs, ls, _ = x_sample.shape
    depth = ln_g.shape[0]
    alpha = (2 * depth) ** 0.25
    tp = bp * lp
    x = jnp.concatenate([x_prompt.reshape(tp, d), x_sample.reshape(bs * ls, d)], axis=0)
    xb = x.astype(BF16)

    ssd_q_prompt = _tile(lp, 256, GDN_CHUNK)
    ssd_q_sample = _tile(ls, 256, GDN_CHUNK)
    if tp % ssd_q_sample or tp % ssd_q_prompt:
        raise ValueError("prompt rows must be chunk aligned")
    gdn_cps_prompt = _tile(lp // GDN_CHUNK, 2, 1)
    gdn_cps_sample = _tile(ls // GDN_CHUNK, 2, 1)
    ssd_groups = [(0, bp, lp, ssd_q_prompt), (tp, bs, ls, ssd_q_sample)]
    gdn_groups = [(0, bp, lp, gdn_cps_prompt), (tp, bs, ls, gdn_cps_sample)]

    ssd_h, ssd_c, gdn_h, gdn_c = [], [], [], []
    for i in range(depth):
        j = i // 2
        if i % 2 == 0:
            zero_h = jnp.zeros((bp,) + state_ssd.shape[2:], F32)
            zero_c = jnp.zeros((bp,) + cache_ssd_conv.shape[2:], F32)
            mix, hs, cs = _ssd_layer(xb, ssd_groups, [zero_h, state_ssd[j]], [zero_c, cache_ssd_conv[j]],
                                     ssd_w_in[j], ssd_conv_w[j], ssd_conv_b[j], ssd_dt_bias[j], ssd_a_log[j],
                                     ssd_d[j], ssd_norm_g[j], ssd_w_out[j])
            ssd_h.append(hs)
            ssd_c.append(cs)
        else:
            zero_h = jnp.zeros((bp,) + state_gdn.shape[2:], F32)
            zero_c = jnp.zeros((bp,) + cache_gdn_conv.shape[2:], F32)
            mix, hs, cs = _gdn_layer(xb, gdn_groups, [zero_h, state_gdn[j]], [zero_c, cache_gdn_conv[j]],
                                     gdn_w_in[j], gdn_conv_w[j], gdn_a_log[j], gdn_dt_bias[j], gdn_norm_g[j],
                                     gdn_w_out[j])
            gdn_h.append(hs)
            gdn_c.append(cs)
        x, xb = _deepnorm_ln(x, mix, ln_g[i, 0], ln_b[i, 0], alpha)
        x, xb = _moe_layer(x, i, moe_w_router, moe_b_router, moe_w_gate_up, moe_b_gate_up, moe_w_down,
                           moe_b_down, ln_g, ln_b, alpha)

    def stack(items, which):
        return jnp.stack([it[which] for it in items])

    y_prompt = x[:tp].reshape(bp, lp, d)
    y_sample = x[tp:].reshape(bs, ls, d)
    return (y_prompt, y_sample,
            stack(ssd_h, 0), stack(ssd_c, 0), stack(gdn_h, 0), stack(gdn_c, 0),
            stack(ssd_h, 1), stack(ssd_c, 1), stack(gdn_h, 1), stack(gdn_c, 1))
```
